```python
import math
import jax, jax.numpy as jnp
from jax import lax
import numpy as np

D_MODEL = 1024
BATCH = 16
SEQ = 2048
DEPTH = 4
DEC_BATCH = 32
DEC_SEQ = 64
PAST_LEN = 1024

CHUNK = 64
D_MIX = D_MODEL
D_RET = D_MIX // 2
D_SSM = D_MIX // 4
D_LRU = D_MIX - D_RET - D_SSM
RET_HEADS = 4
RET_HEAD_DIM = D_RET // RET_HEADS
ROPE_BASE = 10000.0
SSM_GROUP = 16
SSM_GROUPS = D_SSM // SSM_GROUP
SSM_STATE = 64
LRU_BLOCKS = 4
LRU_BLOCK = D_LRU // LRU_BLOCKS
CONV_W = 4
LRU_C = 8.0
D_FFN = ((8 * D_MODEL // 3 + 255) // 256) * 256
D_IN = 4 * D_RET + D_SSM + 2 * D_LRU
_SPLITS = (D_RET, 2 * D_RET, 3 * D_RET, 4 * D_RET, 4 * D_RET + D_SSM, 4 * D_RET + D_SSM + D_LRU)
EPS = 1e-6
GN_EPS = 1e-5

kernel_name = "hybrid_retention_s5_rglru_stream_step"


def _rmsnorm(x, g):
    xf = x.astype(jnp.float32)
    y = xf * lax.rsqrt(jnp.mean(xf * xf, axis=-1, keepdims=True) + EPS)
    return (y * g.astype(jnp.float32)).astype(x.dtype)


def _rms(x):
    return x * lax.rsqrt(jnp.mean(x * x, axis=-1, keepdims=True) + EPS)


def _group_norm(o):
    mu = jnp.mean(o, axis=-1, keepdims=True)
    oc = o - mu
    return oc * lax.rsqrt(jnp.mean(oc * oc, axis=-1, keepdims=True) + GN_EPS)


def _rotary(x, pos):
    d = x.shape[-1]
    inv_freq = ROPE_BASE ** (-jnp.arange(0, d, 2, dtype=jnp.float32) / d)
    ang = pos.astype(jnp.float32)[:, None] * inv_freq[None, :]
    cos = jnp.cos(ang)[None, :, None, :]
    sin = jnp.sin(ang)[None, :, None, :]
    x1, x2 = x[..., : d // 2], x[..., d // 2:]
    return jnp.concatenate([x1 * cos - x2 * sin, x2 * cos + x1 * sin], axis=-1)


def _linear_scan(a, b):
    def comb(e1, e2):
        a1, b1 = e1
        a2, b2 = e2
        return a1 * a2, a2 * b1 + b2
    _, h = lax.associative_scan(comb, (a, b), axis=1)
    return h


def _retention(q, k, v, s0):
    b, l, h, d = q.shape
    cs = min(CHUNK, l)
    nc = l // cs
    gamma = 1.0 - 2.0 ** (-5.0 - jnp.arange(h, dtype=jnp.float32))
    log_g = jnp.log(gamma)
    idx = jnp.arange(cs, dtype=jnp.float32)
    intra = jnp.exp(log_g[:, None, None] * jnp.abs(idx[:, None] - idx[None, :]))
    q = q.reshape(b, nc, cs, h, d)
    k = k.reshape(b, nc, cs, h, d)
    v = v.reshape(b, nc, cs, h, d)
    scores = jnp.einsum('bnihd,bnjhd->bnhij', q, k) * intra
    o_intra = jnp.einsum('bnhij,bnjhd->bnihd', scores, v)
    k_dec = jnp.exp(log_g[None, :] * (cs - 1.0 - idx)[:, None])
    u = jnp.einsum('bnjhk,bnjhv,jh->nbhkv', k, v, k_dec)
    g_chunk = jnp.exp(log_g * cs)[:, None, None]

    def step(s, u_c):
        return g_chunk * s + u_c, s

    s_final, s_prev = lax.scan(step, s0, u)
    q_dec = jnp.exp(log_g[None, :] * (idx + 1.0)[:, None])
    o_cross = jnp.einsum('bnihk,nbhkv,ih->bnihv', q, s_prev, q_dec)
    return (o_intra + o_cross).reshape(b, l, h, d), s_final


def _s5(u, h0, a_re, a_im, b_re, b_im, c_re, c_im, d_skip, log_dt, w_glu):
    bsz, l, _ = u.shape
    uf = u.reshape(bsz, l, SSM_GROUPS, SSM_GROUP)
    lam = lax.complex(a_re.astype(jnp.float32), a_im.astype(jnp.float32))
    dt = jnp.exp(log_dt.astype(jnp.float32))[:, None]
    lam_bar = jnp.exp(lam * dt)
    bmat = lax.complex(b_re.astype(jnp.float32), b_im.astype(jnp.float32))
    b_bar = ((lam_bar - 1.0) / lam)[..., None] * bmat
    bu = jnp.einsum('blgc,gpc->blgp', uf, b_bar)
    bu = bu.at[:, 0].add(lam_bar[None] * h0)
    h = _linear_scan(jnp.broadcast_to(lam_bar, bu.shape), bu)
    cmat = lax.complex(c_re.astype(jnp.float32), c_im.astype(jnp.float32))
    y = jnp.einsum('blgp,gcp->blgc', h, cmat).real + d_skip.astype(jnp.float32).reshape(SSM_GROUPS, SSM_GROUP) * uf
    y = jax.nn.gelu(y.reshape(bsz, l, D_SSM))
    ga = jnp.matmul(y, w_glu.astype(jnp.float32))
    return ga[..., :D_SSM] * jax.nn.sigmoid(ga[..., D_SSM:]), h[:, -1]


def _rglru(x_in, gate, conv_buf, h0, conv_w, conv_b, w_a, b_a, w_x, b_x, lam):
    bsz, l, _ = x_in.shape
    padded = jnp.concatenate([conv_buf.astype(jnp.float32), x_in], axis=1)
    cw = conv_w.astype(jnp.float32)
    xc = conv_b.astype(jnp.float32) + sum(padded[:, t:t + l] * cw[t] for t in range(CONV_W))
    new_buf = padded[:, -(CONV_W - 1):]
    xb = xc.reshape(bsz, l, LRU_BLOCKS, LRU_BLOCK)
    r = jax.nn.sigmoid(jnp.einsum('blhi,hij->blhj', xb, w_a.astype(jnp.float32)).reshape(bsz, l, D_LRU) + b_a.astype(jnp.float32))
    ig = jax.nn.sigmoid(jnp.einsum('blhi,hij->blhj', xb, w_x.astype(jnp.float32)).reshape(bsz, l, D_LRU) + b_x.astype(jnp.float32))
    log_a = -LRU_C * r * jax.nn.softplus(-lam.astype(jnp.float32))
    a = jnp.exp(log_a)
    bx = jnp.sqrt(-jnp.expm1(2.0 * log_a)) * (ig * xc)
    bx = bx.at[:, 0].add(a[:, 0] * h0.astype(jnp.float32))
    h = _linear_scan(a, bx)
    return h * jax.nn.gelu(gate), h[:, -1], new_buf


def _trunk(x, pos_offset, s_ret, s_ssm, s_lru, c_conv, p):
    b, l, _ = x.shape
    pos = pos_offset + jnp.arange(l, dtype=jnp.int32)
    new_ret, new_ssm, new_lru, new_conv = [], [], [], []
    for i in range(DEPTH):
        h = _rmsnorm(x, p['norm_mix'][i])
        proj = jnp.matmul(h, p['w_in'][i]).astype(jnp.float32)
        q, k, v, g, u_ssm, x_lru, gate_lru = jnp.split(proj, _SPLITS, axis=-1)
        qh = _rotary(q.reshape(b, l, RET_HEADS, RET_HEAD_DIM), pos) * (RET_HEAD_DIM ** -0.5)
        kh = _rotary(k.reshape(b, l, RET_HEADS, RET_HEAD_DIM), pos)
        vh = v.reshape(b, l, RET_HEADS, RET_HEAD_DIM)
        o_ret, sr = _retention(qh, kh, vh, s_ret[i].astype(jnp.float32))
        o_ret = _group_norm(o_ret).reshape(b, l, D_RET) * jax.nn.silu(g)
        st = s_ssm[i].astype(jnp.float32)
        h0 = lax.complex(st[..., 0], st[..., 1])
        o_ssm, hs = _s5(u_ssm, h0, p['ssm_a_re'][i], p['ssm_a_im'][i], p['ssm_b_re'][i], p['ssm_b_im'][i],
                        p['ssm_c_re'][i], p['ssm_c_im'][i], p['ssm_d'][i], p['ssm_log_dt'][i], p['ssm_w_glu'][i])
        o_lru, hl, cb = _rglru(x_lru, gate_lru, c_conv[i], s_lru[i], p['lru_conv_w'][i], p['lru_conv_b'][i],
                               p['lru_w_a'][i], p['lru_b_a'][i], p['lru_w_x'][i], p['lru_b_x'][i], p['lru_lambda'][i])
        mixed = jnp.concatenate([_rms(o_ret), _rms(o_ssm), _rms(o_lru)], axis=-1) * p['mix_scale'][i].astype(jnp.float32)
        x = x + jnp.matmul(mixed, p['w_out'][i].astype(jnp.float32)).astype(x.dtype)
        h = _rmsnorm(x, p['norm_ffn'][i])
        ff = jax.nn.silu(jnp.matmul(h, p['w_ffn_gate'][i])) * jnp.matmul(h, p['w_ffn_up'][i])
        x = x + jnp.matmul(ff, p['w_ffn_down'][i]).astype(x.dtype)
        new_ret.append(sr)
        new_ssm.append(jnp.stack([hs.real, hs.imag], axis=-1))
        new_lru.append(hl)
        new_conv.append(cb)
    y = _rmsnorm(x, p['norm_final'])
    return y, jnp.stack(new_ret), jnp.stack(new_ssm), jnp.stack(new_lru), jnp.stack(new_conv)


def setup_inputs(seed: int = 0) -> dict:
    key = jax.random.key(seed)
    ks = jax.random.split(key, 32)
    f32 = jnp.float32

    def nrm(k, shape, scale):
        return jax.random.normal(k, shape, f32) * scale

    n_idx = jnp.arange(SSM_STATE, dtype=f32)
    a_target = jax.random.uniform(ks[26], (DEPTH, D_LRU), f32, 0.9, 0.999)
    sig = a_target ** (1.0 / LRU_C)
    return {
        'x_prompt': nrm(ks[0], (BATCH, SEQ, D_MODEL), 1.0),
        'x_sample': nrm(ks[1], (DEC_BATCH, DEC_SEQ, D_MODEL), 1.0),
        'state_ret': nrm(ks[2], (DEPTH, DEC_BATCH, RET_HEADS, RET_HEAD_DIM, RET_HEAD_DIM), 0.5),
        'state_ssm': nrm(ks[3], (DEPTH, DEC_BATCH, SSM_GROUPS, SSM_STATE, 2), 0.5),
        'state_lru': nrm(ks[4], (DEPTH, DEC_BATCH, D_LRU), 0.5),
        'cache_conv': nrm(ks[5], (DEPTH, DEC_BATCH, CONV_W - 1, D_LRU), 1.0),
        'norm_mix': 1.0 + nrm(ks[6], (DEPTH, D_MODEL), 0.01),
        'w_in': nrm(ks[7], (DEPTH, D_MODEL, D_IN), D_MODEL ** -0.5),
        'mix_scale': 1.0 + nrm(ks[8], (DEPTH, D_MIX), 0.01),
        'w_out': nrm(ks[9], (DEPTH, D_MIX, D_MODEL), D_MIX ** -0.5),
        'ssm_a_re': -0.5 + nrm(ks[10], (DEPTH, SSM_GROUPS, SSM_STATE), 0.01),
        'ssm_a_im': math.pi * n_idx + nrm(ks[11], (DEPTH, SSM_GROUPS, SSM_STATE), 0.01),
        'ssm_b_re': nrm(ks[12], (DEPTH, SSM_GROUPS, SSM_STATE, SSM_GROUP), (2.0 * SSM_GROUP) ** -0.5),
        'ssm_b_im': nrm(ks[13], (DEPTH, SSM_GROUPS, SSM_STATE, SSM_GROUP), (2.0 * SSM_GROUP) ** -0.5),
        'ssm_c_re': nrm(ks[14], (DEPTH, SSM_GROUPS, SSM_GROUP, SSM_STATE), (2.0 * SSM_STATE) ** -0.5),
        'ssm_c_im': nrm(ks[15], (DEPTH, SSM_GROUPS, SSM_GROUP, SSM_STATE), (2.0 * SSM_STATE) ** -0.5),
        'ssm_d': nrm(ks[16], (DEPTH, D_SSM), 1.0),
        'ssm_log_dt': jax.random.uniform(ks[17], (DEPTH, SSM_GROUPS), f32, math.log(1e-3), math.log(1e-1)),
        'ssm_w_glu': nrm(ks[18], (DEPTH, D_SSM, 2 * D_SSM), D_SSM ** -0.5),
        'lru_conv_w': nrm(ks[19], (DEPTH, CONV_W, D_LRU), 0.5),
        'lru_conv_b': nrm(ks[20], (DEPTH, D_LRU), 0.01),
        'lru_w_a': nrm(ks[21], (DEPTH, LRU_BLOCKS, LRU_BLOCK, LRU_BLOCK), LRU_BLOCK ** -0.5),
        'lru_b_a': nrm(ks[22], (DEPTH, D_LRU), 0.01),
        'lru_w_x': nrm(ks[23], (DEPTH, LRU_BLOCKS, LRU_BLOCK, LRU_BLOCK), LRU_BLOCK ** -0.5),
        'lru_b_x': nrm(ks[24], (DEPTH, D_LRU), 0.01),
        'lru_lambda': jnp.log(sig) - jnp.log1p(-sig),
        'norm_ffn': 1.0 + nrm(ks[25], (DEPTH, D_MODEL), 0.01),
        'w_ffn_gate': nrm(ks[27], (DEPTH, D_MODEL, D_FFN), D_MODEL ** -0.5),
        'w_ffn_up': nrm(ks[28], (DEPTH, D_MODEL, D_FFN), D_MODEL ** -0.5),
        'w_ffn_down': nrm(ks[29], (DEPTH, D_FFN, D_MODEL), D_FFN ** -0.5),
        'norm_final': 1.0 + nrm(ks[30], (D_MODEL,), 0.01),
    }


def reference(x_prompt, x_sample, state_ret, state_ssm, state_lru, cache_conv,
              norm_mix, w_in, mix_scale, w_out,
              ssm_a_re, ssm_a_im, ssm_b_re, ssm_b_im, ssm_c_re, ssm_c_im, ssm_d, ssm_log_dt, ssm_w_glu,
              lru_conv_w, lru_conv_b, lru_w_a, lru_b_a, lru_w_x, lru_b_x, lru_lambda,
              norm_ffn, w_ffn_gate, w_ffn_up, w_ffn_down, norm_final):
    p = {
        'norm_mix': norm_mix, 'w_in': w_in, 'mix_scale': mix_scale, 'w_out': w_out,
        'ssm_a_re': ssm_a_re, 'ssm_a_im': ssm_a_im, 'ssm_b_re': ssm_b_re, 'ssm_b_im': ssm_b_im,
        'ssm_c_re': ssm_c_re, 'ssm_c_im': ssm_c_im, 'ssm_d': ssm_d, 'ssm_log_dt': ssm_log_dt,
        'ssm_w_glu': ssm_w_glu,
        'lru_conv_w': lru_conv_w, 'lru_conv_b': lru_conv_b, 'lru_w_a': lru_w_a, 'lru_b_a': lru_b_a,
        'lru_w_x': lru_w_x, 'lru_b_x': lru_b_x, 'lru_lambda': lru_lambda,
        'norm_ffn': norm_ffn, 'w_ffn_gate': w_ffn_gate, 'w_ffn_up': w_ffn_up, 'w_ffn_down': w_ffn_down,
        'norm_final': norm_final,
    }
    bp = x_prompt.shape[0]
    z_ret = jnp.zeros((DEPTH, bp, RET_HEADS, RET_HEAD_DIM, RET_HEAD_DIM), jnp.float32)
    z_ssm = jnp.zeros((DEPTH, bp, SSM_GROUPS, SSM_STATE, 2), jnp.float32)
    z_lru = jnp.zeros((DEPTH, bp, D_LRU), jnp.float32)
    z_conv = jnp.zeros((DEPTH, bp, CONV_W - 1, D_LRU), jnp.float32)
    y_prompt, p_ret, p_ssm, p_lru, p_conv = _trunk(x_prompt, 0, z_ret, z_ssm, z_lru, z_conv, p)
    y_sample, s_ret, s_ssm, s_lru, s_conv = _trunk(x_sample, PAST_LEN, state_ret, state_ssm, state_lru, cache_conv, p)
    return (y_prompt, y_sample, p_ret, p_ssm, p_lru, p_conv, s_ret, s_ssm, s_lru, s_conv)
```

```python
import functools
import math

import jax
import jax.numpy as jnp
from jax import lax
from jax.experimental import pallas as pl
from jax.experimental.pallas import tpu as pltpu

F32 = jnp.float32
BF16 = jnp.bfloat16

D_MODEL = 1024
D_RET = 512
D_SSM = 256
D_LRU = 256
RET_HEADS = 4
HEAD_DIM = 128
RET_CHUNK = 64
ROPE_BASE = 10000.0
SSM_GROUP = 16
SSM_GROUPS = 16
SSM_STATE = 64
N_SSM = SSM_GROUPS * SSM_STATE
LRU_BLOCKS = 4
CONV_W = 4
LRU_C = 8.0
D_FFN = 2816
D_IN = 4 * D_RET + D_SSM + 2 * D_LRU
D_SCAN = D_SSM + 2 * D_LRU
EPS = 1e-6
GN_EPS = 1e-5

LANES = 128
SEQ_TILE = 8
ROW_BLOCK = 512
RET_BLOCK = 256
SCAN_BLOCK = 64
VMEM_LIMIT = 56 * 1024 * 1024


def _const_spec(shape):
    nd = len(shape)
    return pl.BlockSpec(shape, lambda *_: (0,) * nd, pipeline_mode=pl.Buffered(1))


def _sigmoid(x):
    return 1.0 / (1.0 + jnp.exp(-x))


def _gelu_tanh(x):
    return 0.5 * x * (1.0 + jnp.tanh(math.sqrt(2.0 / math.pi) * (x + 0.044715 * (x * x * x))))


def _rms_rows(x, eps):
    return x * lax.rsqrt(jnp.mean(x * x, axis=-1, keepdims=True) + eps)


def _proj_kernel(x_ref, nw_ref, w_ref, rot_ref, qkv_ref, g_ref, sc_ref):
    x = x_ref[...]
    h = (_rms_rows(x, EPS) * nw_ref[...]).astype(BF16)
    proj = jnp.dot(h, w_ref[...], preferred_element_type=F32)
    cos2 = rot_ref[:, :HEAD_DIM]
    sin2 = rot_ref[:, HEAD_DIM:]
    q_scale = HEAD_DIM ** -0.5
    for head in range(2 * RET_HEADS):
        lo = head * HEAD_DIM
        xh = proj[:, lo:lo + HEAD_DIM]
        r = xh * cos2 + pltpu.roll(xh, HEAD_DIM // 2, axis=1) * sin2
        if head < RET_HEADS:
            r = r * q_scale
        qkv_ref[:, lo:lo + HEAD_DIM] = r.astype(BF16)
    qkv_ref[:, 2 * D_RET:3 * D_RET] = proj[:, 2 * D_RET:3 * D_RET].astype(BF16)
    g_ref[...] = proj[:, 3 * D_RET:4 * D_RET]
    sc_ref[...] = proj[:, 4 * D_RET:]


def _proj_call(x2d, nw, w_in, rot, m, nper):
    n = x2d.shape[0]
    return pl.pallas_call(
        _proj_kernel,
        grid=(n // m,),
        in_specs=[
            pl.BlockSpec((m, D_MODEL), lambda i: (i, 0)),
            _const_spec((1, D_MODEL)),
            _const_spec((D_MODEL, D_IN)),
            pl.BlockSpec((m, 2 * HEAD_DIM), lambda i: (i % nper, 0)),
        ],
        out_specs=[
            pl.BlockSpec((m, 3 * D_RET), lambda i: (i, 0)),
            pl.BlockSpec((m, D_RET), lambda i: (i, 0)),
            pl.BlockSpec((m, D_SCAN), lambda i: (i, 0)),
        ],
        out_shape=[
            jax.ShapeDtypeStruct((n, 3 * D_RET), BF16),
            jax.ShapeDtypeStruct((n, D_RET), F32),
            jax.ShapeDtypeStruct((n, D_SCAN), F32),
        ],
        compiler_params=pltpu.CompilerParams(
            dimension_semantics=("arbitrary",), vmem_limit_bytes=VMEM_LIMIT),
        name="proj",
    )(x2d, nw, w_in, rot)


def _ret_kernel(qkv_ref, g_ref, s0_ref, dmask_ref, qdec_ref, kdec_ref, gblk_ref,
                o_ref, sfin_ref, s_scr):
    c = pl.program_id(1)

    @pl.when(c == 0)
    def _():
        s_scr[...] = s0_ref[...]

    for h in range(RET_HEADS):
        lo = h * HEAD_DIM
        q = qkv_ref[:, lo:lo + HEAD_DIM]
        k = qkv_ref[:, D_RET + lo:D_RET + lo + HEAD_DIM]
        v = qkv_ref[:, 2 * D_RET + lo:2 * D_RET + lo + HEAD_DIM]
        s = s_scr[h]
        scores = lax.dot_general(q, k, (((1,), (1,)), ((), ())), preferred_element_type=F32)
        a = (scores * dmask_ref[h]).astype(BF16)
        o = jnp.dot(a, v, preferred_element_type=F32)
        o = o + jnp.dot(q, s.astype(BF16), preferred_element_type=F32) * qdec_ref[h]
        kd = (k.astype(F32) * kdec_ref[h]).astype(BF16)
        u = lax.dot_general(kd, v, (((0,), (0,)), ((), ())), preferred_element_type=F32)
        s_scr[h] = gblk_ref[h] * s + u
        oc = o - jnp.mean(o, axis=-1, keepdims=True)
        on = oc * lax.rsqrt(jnp.mean(oc * oc, axis=-1, keepdims=True) + GN_EPS)
        gg = g_ref[:, lo:lo + HEAD_DIM]
        o_ref[:, lo:lo + HEAD_DIM] = on * (gg * _sigmoid(gg))

    @pl.when(c == pl.num_programs(1) - 1)
    def _():
        sfin_ref[...] = s_scr[...]


def _ret_tables(tb):
    h = jnp.arange(RET_HEADS, dtype=F32)
    gamma = 1.0 - 2.0 ** (-5.0 - h)
    log_g = jnp.log(gamma)
    idx = jnp.arange(tb, dtype=F32)
    chunk = jnp.arange(tb, dtype=jnp.int32) // RET_CHUNK
    diff = idx[:, None] - idx[None, :]
    same = chunk[:, None] == chunk[None, :]
    earlier = chunk[:, None] > chunk[None, :]
    expo = jnp.where(same, jnp.abs(diff), diff)
    dmask = jnp.where((same | earlier)[None], jnp.exp(log_g[:, None, None] * expo[None]), 0.0)
    qdec = jnp.exp(log_g[:, None] * (idx + 1.0)[None, :])
    kdec = jnp.exp(log_g[:, None] * (tb - 1.0 - idx)[None, :])
    gblk = jnp.exp(log_g * tb)
    wide = lambda t: jnp.broadcast_to(t[:, :, None], t.shape + (HEAD_DIM,))
    return dmask, wide(qdec), wide(kdec), jnp.broadcast_to(gblk[:, None, None], (RET_HEADS, 1, HEAD_DIM))


def _ret_call(qkv, g, s0, tables, tb):
    b, l, _ = qkv.shape
    dmask, qdec, kdec, gblk = tables
    state_spec = pl.BlockSpec((None, RET_HEADS, HEAD_DIM, HEAD_DIM), lambda i, c: (i, 0, 0, 0))
    return pl.pallas_call(
        _ret_kernel,
        grid=(b, l // tb),
        in_specs=[
            pl.BlockSpec((None, tb, 3 * D_RET), lambda i, c: (i, c, 0)),
            pl.BlockSpec((None, tb, D_RET), lambda i, c: (i, c, 0)),
            state_spec,
            _const_spec((RET_HEADS, tb, tb)),
            _const_spec((RET_HEADS, tb, HEAD_DIM)),
            _const_spec((RET_HEADS, tb, HEAD_DIM)),
            _const_spec((RET_HEADS, 1, HEAD_DIM)),
        ],
        out_specs=[
            pl.BlockSpec((None, tb, D_RET), lambda i, c: (i, c, 0)),
            state_spec,
        ],
        out_shape=[
            jax.ShapeDtypeStruct((b, l, D_RET), F32),
            jax.ShapeDtypeStruct((b, RET_HEADS, HEAD_DIM, HEAD_DIM), F32),
        ],
        scratch_shapes=[pltpu.VMEM((RET_HEADS, HEAD_DIM, HEAD_DIM), F32)],
        compiler_params=pltpu.CompilerParams(
            dimension_semantics=("arbitrary", "arbitrary"), vmem_limit_bytes=VMEM_LIMIT),
        name="ret",
    )(qkv, g, s0, dmask, qdec, kdec, gblk)


def _scan_kernel(sc_ref, hs0_ref, hl0_ref, cc0_ref,
                 bmat_ref, lam_ref, cmat_ref, dskip_ref, wglu_ref,
                 convw_ref, convb_ref, wa_ref, ba_ref, wx_ref, bxb_ref, sp_ref,
                 o_ref, hs_ref, hl_ref, cc_ref,
                 tm, hbuf, hst, xpad, abuf, bbuf, lst, res, *, tt):
    c = pl.program_id(1)
    rows = tt * SEQ_TILE
    pad_rows = (CONV_W - 1) * SEQ_TILE

    @pl.when(c == 0)
    def _():
        hst[...] = hs0_ref[...]
        lst[...] = hl0_ref[...]
        for j in range(CONV_W - 1):
            xpad[j * SEQ_TILE:(j + 1) * SEQ_TILE, :] = cc0_ref[j]

    for s in range(SEQ_TILE):
        for j in range(D_SCAN // LANES):
            tm[j, pl.ds(s, tt, stride=SEQ_TILE), :] = sc_ref[s, :, j * LANES:(j + 1) * LANES]

    u = jnp.concatenate([tm[0], tm[1]], axis=-1)
    hbuf[...] = jnp.dot(u.astype(BF16), bmat_ref[...], preferred_element_type=F32)

    xpad[pad_rows:, :] = jnp.concatenate([tm[2], tm[3]], axis=-1)
    xc = convb_ref[...] + xpad[0:rows, :] * convw_ref[0:1, :]
    for j in range(1, CONV_W):
        xc = xc + xpad[j * SEQ_TILE:j * SEQ_TILE + rows, :] * convw_ref[j:j + 1, :]
    xcb = xc.astype(BF16)
    r = _sigmoid(jnp.dot(xcb, wa_ref[...], preferred_element_type=F32) + ba_ref[...])
    ig = _sigmoid(jnp.dot(xcb, wx_ref[...], preferred_element_type=F32) + bxb_ref[...])
    log_a = (-LRU_C) * r * sp_ref[...]
    a = jnp.exp(log_a)
    abuf[...] = a
    bbuf[...] = jnp.sqrt(1.0 - a * a) * (ig * xc)

    lam_re = lam_ref[0:1, :]
    lam_im = lam_ref[1:2, :]

    def step(t, carry):
        hr, hi, hl = carry
        off = pl.multiple_of(t * SEQ_TILE, SEQ_TILE)
        br = hbuf[pl.ds(off, SEQ_TILE), 0:N_SSM]
        bi = hbuf[pl.ds(off, SEQ_TILE), N_SSM:2 * N_SSM]
        nr = lam_re * hr - lam_im * hi + br
        ni = lam_re * hi + lam_im * hr + bi
        hbuf[pl.ds(off, SEQ_TILE), 0:N_SSM] = nr
        hbuf[pl.ds(off, SEQ_TILE), N_SSM:2 * N_SSM] = ni
        nl = abuf[pl.ds(off, SEQ_TILE), :] * hl + bbuf[pl.ds(off, SEQ_TILE), :]
        bbuf[pl.ds(off, SEQ_TILE), :] = nl
        return nr, ni, nl

    hr, hi, hl = lax.fori_loop(0, tt, step, (hst[:, 0:N_SSM], hst[:, N_SSM:2 * N_SSM], lst[...]))
    hst[:, 0:N_SSM] = hr
    hst[:, N_SSM:2 * N_SSM] = hi
    lst[...] = hl
    for j in range(CONV_W - 1):
        xpad[j * SEQ_TILE:(j + 1) * SEQ_TILE, :] = xpad[rows + j * SEQ_TILE:rows + (j + 1) * SEQ_TILE, :]

    y = jnp.dot(hbuf[...].astype(BF16), cmat_ref[...], preferred_element_type=F32) + dskip_ref[...] * u
    y = _gelu_tanh(y)
    ga = jnp.dot(y.astype(BF16), wglu_ref[...], preferred_element_type=F32)
    o_ssm = ga[:, 0:D_SSM] * _sigmoid(ga[:, D_SSM:2 * D_SSM])
    o_lru = bbuf[...] * _gelu_tanh(jnp.concatenate([tm[4], tm[5]], axis=-1))
    res[0] = o_ssm[:, :LANES]
    res[1] = o_ssm[:, LANES:]
    res[2] = o_lru[:, :LANES]
    res[3] = o_lru[:, LANES:]

    for s in range(SEQ_TILE):
        for j in range((D_SSM + D_LRU) // LANES):
            o_ref[s, :, j * LANES:(j + 1) * LANES] = res[j, pl.ds(s, tt, stride=SEQ_TILE), :]

    @pl.when(c == pl.num_programs(1) - 1)
    def _():
        hs_ref[...] = hst[...]
        hl_ref[...] = lst[...]
        for j in range(CONV_W - 1):
            cc_ref[j] = xpad[j * SEQ_TILE:(j + 1) * SEQ_TILE, :]


def _scan_call(sc, hs0, hl0, cc0, sp, tt):
    b, l, _ = sc.shape
    rows = tt * SEQ_TILE
    seq_spec = lambda w: pl.BlockSpec((SEQ_TILE, w), lambda i, c: (i, 0))
    cc_spec = pl.BlockSpec((CONV_W - 1, SEQ_TILE, D_LRU), lambda i, c: (0, i, 0))
    return pl.pallas_call(
        functools.partial(_scan_kernel, tt=tt),
        grid=(b // SEQ_TILE, l // tt),
        in_specs=[
            pl.BlockSpec((SEQ_TILE, tt, D_SCAN), lambda i, c: (i, c, 0)),
            seq_spec(2 * N_SSM), seq_spec(D_LRU), cc_spec,
            _const_spec((D_SSM, 2 * N_SSM)), _const_spec((2, N_SSM)), _const_spec((2 * N_SSM, D_SSM)),
            _const_spec((1, D_SSM)), _const_spec((D_SSM, 2 * D_SSM)),
            _const_spec((CONV_W, D_LRU)), _const_spec((1, D_LRU)),
            _const_spec((D_LRU, D_LRU)), _const_spec((1, D_LRU)),
            _const_spec((D_LRU, D_LRU)), _const_spec((1, D_LRU)), _const_spec((1, D_LRU)),
        ],
        out_specs=[
            pl.BlockSpec((SEQ_TILE, tt, D_SSM + D_LRU), lambda i, c: (i, c, 0)),
            seq_spec(2 * N_SSM), seq_spec(D_LRU), cc_spec,
        ],
        out_shape=[
            jax.ShapeDtypeStruct((b, l, D_SSM + D_LRU), F32),
            jax.ShapeDtypeStruct((b, 2 * N_SSM), F32),
            jax.ShapeDtypeStruct((b, D_LRU), F32),
            jax.ShapeDtypeStruct((CONV_W - 1, b, D_LRU), F32),
        ],
        scratch_shapes=[
            pltpu.VMEM((D_SCAN // LANES, rows, LANES), F32),
            pltpu.VMEM((rows, 2 * N_SSM), F32),
            pltpu.VMEM((SEQ_TILE, 2 * N_SSM), F32),
            pltpu.VMEM((rows + (CONV_W - 1) * SEQ_TILE, D_LRU), F32),
            pltpu.VMEM((rows, D_LRU), F32),
            pltpu.VMEM((rows, D_LRU), F32),
            pltpu.VMEM((SEQ_TILE, D_LRU), F32),
            pltpu.VMEM(((D_SSM + D_LRU) // LANES, rows, LANES), F32),
        ],
        compiler_params=pltpu.CompilerParams(
            dimension_semantics=("arbitrary", "arbitrary"), vmem_limit_bytes=VMEM_LIMIT),
        name="scan",
    )(sc, hs0, hl0, cc0, *sp)


def _scan_params(a_re, a_im, b_re, b_im, c_re, c_im, d_skip, log_dt, w_glu,
                 conv_w, conv_b, w_a, b_a, w_x, b_x, lam):
    dt = jnp.exp(log_dt)[:, None]
    mag = jnp.exp(a_re * dt)
    lb_re = mag * jnp.cos(a_im * dt)
    lb_im = mag * jnp.sin(a_im * dt)
    den = a_re * a_re + a_im * a_im
    f_re = ((lb_re - 1.0) * a_re + lb_im * a_im) / den
    f_im = (lb_im * a_re - (lb_re - 1.0) * a_im) / den
    bb_re = f_re[..., None] * b_re - f_im[..., None] * b_im
    bb_im = f_re[..., None] * b_im + f_im[..., None] * b_re
    eye_g = jnp.eye(SSM_GROUPS, dtype=F32)
    bd_in = lambda t: jnp.einsum('gpc,gh->gchp', t, eye_g).reshape(D_SSM, N_SSM)
    bmat = jnp.concatenate([bd_in(bb_re), bd_in(bb_im)], axis=1).astype(BF16)
    bd_out = lambda t: jnp.einsum('gcp,gh->gphc', t, eye_g).reshape(N_SSM, D_SSM)
    cmat = jnp.concatenate([bd_out(c_re), bd_out(-c_im)], axis=0).astype(BF16)
    lam_bar = jnp.stack([lb_re.reshape(N_SSM), lb_im.reshape(N_SSM)])
    eye_b = jnp.eye(LRU_BLOCKS, dtype=F32)
    bd_lru = lambda t: jnp.einsum('hij,hk->hikj', t, eye_b).reshape(D_LRU, D_LRU).astype(BF16)
    row = lambda t: t.reshape(1, -1)
    return (bmat, lam_bar, cmat, row(d_skip), w_glu.astype(BF16),
            conv_w, row(conv_b), bd_lru(w_a), row(b_a), bd_lru(w_x), row(b_x),
            row(jax.nn.softplus(-lam)))


def _merge_kernel(x_ref, oret_ref, osl_ref, ms_ref, wo_ref, nf_ref, wg_ref, wu_ref, wd_ref, nfin_ref,
                  y_ref, *, final):
    x = x_ref[...]
    osl = osl_ref[...]
    mixed = jnp.concatenate(
        [_rms_rows(oret_ref[...], EPS), _rms_rows(osl[:, :D_SSM], EPS), _rms_rows(osl[:, D_SSM:], EPS)],
        axis=-1) * ms_ref[...]
    x = x + jnp.dot(mixed.astype(BF16), wo_ref[...], preferred_element_type=F32)
    h = (_rms_rows(x, EPS) * nf_ref[...]).astype(BF16)
    gate = jnp.dot(h, wg_ref[...], preferred_element_type=F32)
    up = jnp.dot(h, wu_ref[...], preferred_element_type=F32)
    ff = (gate * _sigmoid(gate) * up).astype(BF16)
    x = x + jnp.dot(ff, wd_ref[...], preferred_element_type=F32)
    if final:
        x = _rms_rows(x, EPS) * nfin_ref[...]
    y_ref[...] = x


def _merge_call(x2d, oret, osl, ms, wo, nf, wg, wu, wd, nfin, m, final):
    n = x2d.shape[0]
    row_spec = lambda w: pl.BlockSpec((m, w), lambda i: (i, 0))
    return pl.pallas_call(
        functools.partial(_merge_kernel, final=final),
        grid=(n // m,),
        in_specs=[
            row_spec(D_MODEL), row_spec(D_RET), row_spec(D_SSM + D_LRU),
            _const_spec((1, D_MODEL)), _const_spec((D_MODEL, D_MODEL)), _const_spec((1, D_MODEL)),
            _const_spec((D_MODEL, D_FFN)), _const_spec((D_MODEL, D_FFN)), _const_spec((D_FFN, D_MODEL)),
            _const_spec((1, D_MODEL)),
        ],
        out_specs=row_spec(D_MODEL),
        out_shape=jax.ShapeDtypeStruct((n, D_MODEL), F32),
        compiler_params=pltpu.CompilerParams(
            dimension_semantics=("arbitrary",), vmem_limit_bytes=VMEM_LIMIT),
        name="merge",
    )(x2d, oret, osl, ms, wo, nf, wg, wu, wd, nfin)


def _rotary_table(pos_offset, l, m):
    pos = pos_offset + jnp.arange(l, dtype=jnp.int32)
    inv_freq = ROPE_BASE ** (-jnp.arange(0, HEAD_DIM, 2, dtype=F32) / HEAD_DIM)
    ang = pos.astype(F32)[:, None] * inv_freq[None, :]
    cos, sin = jnp.cos(ang), jnp.sin(ang)
    tab = jnp.concatenate([cos, cos, -sin, sin], axis=-1)
    if l < m:
        tab = jnp.tile(tab, (m // l, 1))
    return tab


def _trunk(x, pos_offset, s_ret, s_ssm, s_lru, c_conv, layers, norm_final):
    b, l, _ = x.shape
    n = b * l
    m = min(ROW_BLOCK, n)
    tb = min(RET_BLOCK, l)
    tt = min(SCAN_BLOCK, l)
    assert n % m == 0 and (l % m == 0 or m % l == 0) and l % tb == 0 and l % tt == 0 and b % SEQ_TILE == 0
    rot = _rotary_table(pos_offset, l, m)
    nper = max(l // m, 1)
    tables = _ret_tables(tb)
    depth = len(layers)
    x2d = x.reshape(n, D_MODEL)
    new_ret, new_ssm, new_lru, new_conv = [], [], [], []
    for i, p in enumerate(layers):
        qkv, g, sc = _proj_call(x2d, p['norm_mix'], p['w_in'], rot, m, nper)
        oret, sr = _ret_call(qkv.reshape(b, l, 3 * D_RET), g.reshape(b, l, D_RET), s_ret[i], tables, tb)
        hs0 = jnp.concatenate([s_ssm[i][..., 0].reshape(b, N_SSM), s_ssm[i][..., 1].reshape(b, N_SSM)], axis=1)
        osl, hs, hl, cc = _scan_call(sc.reshape(b, l, D_SCAN), hs0, s_lru[i],
                                     jnp.swapaxes(c_conv[i], 0, 1), p['scan'], tt)
        x2d = _merge_call(x2d, oret.reshape(n, D_RET), osl.reshape(n, D_SSM + D_LRU),
                          p['mix_scale'], p['w_out'], p['norm_ffn'], p['w_gate'], p['w_up'], p['w_down'],
                          norm_final, m, final=(i == depth - 1))
        new_ret.append(sr)
        new_ssm.append(jnp.stack([hs[:, :N_SSM].reshape(b, SSM_GROUPS, SSM_STATE),
                                  hs[:, N_SSM:].reshape(b, SSM_GROUPS, SSM_STATE)], axis=-1))
        new_lru.append(hl)
        new_conv.append(jnp.swapaxes(cc, 0, 1))
    return (x2d.reshape(b, l, D_MODEL), jnp.stack(new_ret), jnp.stack(new_ssm),
            jnp.stack(new_lru), jnp.stack(new_conv))


def _prepare_layers(norm_mix, w_in, mix_scale, w_out,
                    ssm_a_re, ssm_a_im, ssm_b_re, ssm_b_im, ssm_c_re, ssm_c_im, ssm_d, ssm_log_dt, ssm_w_glu,
                    lru_conv_w, lru_conv_b, lru_w_a, lru_b_a, lru_w_x, lru_b_x, lru_lambda,
                    norm_ffn, w_ffn_gate, w_ffn_up, w_ffn_down):
    layers = []
    for i in range(norm_mix.shape[0]):
        layers.append({
            'norm_mix': norm_mix[i].reshape(1, D_MODEL),
            'w_in': w_in[i].astype(BF16),
            'mix_scale': mix_scale[i].reshape(1, D_MODEL),
            'w_out': w_out[i].astype(BF16),
            'scan': _scan_params(ssm_a_re[i], ssm_a_im[i], ssm_b_re[i], ssm_b_im[i], ssm_c_re[i], ssm_c_im[i],
                                 ssm_d[i], ssm_log_dt[i], ssm_w_glu[i], lru_conv_w[i], lru_conv_b[i],
                                 lru_w_a[i], lru_b_a[i], lru_w_x[i], lru_b_x[i], lru_lambda[i]),
            'norm_ffn': norm_ffn[i].reshape(1, D_MODEL),
            'w_gate': w_ffn_gate[i].astype(BF16),
            'w_up': w_ffn_up[i].astype(BF16),
            'w_down': w_ffn_down[i].astype(BF16),
        })
    return layers


def kernel(x_prompt, x_sample, state_ret, state_ssm, state_lru, cache_conv, norm_mix, w_in, mix_scale, w_out, ssm_a_re, ssm_a_im, ssm_b_re, ssm_b_im, ssm_c_re, ssm_c_im, ssm_d, ssm_log_dt, ssm_w_glu, lru_conv_w, lru_conv_b, lru_w_a, lru_b_a, lru_w_x, lru_b_x, lru_lambda, norm_ffn, w_ffn_gate, w_ffn_up, w_ffn_down, norm_final):
    layers = _prepare_layers(norm_mix, w_in, mix_scale, w_out,
                             ssm_a_re, ssm_a_im, ssm_b_re, ssm_b_im, ssm_c_re, ssm_c_im, ssm_d, ssm_log_dt,
                             ssm_w_glu, lru_conv_w, lru_conv_b, lru_w_a, lru_b_a, lru_w_x, lru_b_x, lru_lambda,
                             norm_ffn, w_ffn_gate, w_ffn_up, w_ffn_down)
    depth = len(layers)
    nfin = norm_final.reshape(1, D_MODEL)
    bp = x_prompt.shape[0]
    z_ret = jnp.zeros((depth, bp, RET_HEADS, HEAD_DIM, HEAD_DIM), F32)
    z_ssm = jnp.zeros((depth, bp, SSM_GROUPS, SSM_STATE, 2), F32)
    z_lru = jnp.zeros((depth, bp, D_LRU), F32)
    z_conv = jnp.zeros((depth, bp, CONV_W - 1, D_LRU), F32)
    past_len = 1024
    out_p = _trunk(x_prompt, 0, z_ret, z_ssm, z_lru, z_conv, layers, nfin)
    out_s = _trunk(x_sample, past_len, state_ret, state_ssm, state_lru, cache_conv, layers, nfin)
    return (out_p[0], out_s[0]) + tuple(out_p[1:]) + tuple(out_s[1:])
```

```python
import functools
import math

import jax
import jax.numpy as jnp
from jax import lax
from jax.experimental import pallas as pl
from jax.experimental.pallas import tpu as pltpu

F32 = jnp.float32
BF16 = jnp.bfloat16

D_MODEL = 1024
D_RET = 512
D_SSM = 256
D_LRU = 256
RET_HEADS = 4
HEAD_DIM = 128
RET_CHUNK = 64
ROPE_BASE = 10000.0
SSM_GROUP = 16
SSM_GROUPS = 16
SSM_STATE = 64
N_SSM = SSM_GROUPS * SSM_STATE
LRU_BLOCKS = 4
CONV_W = 4
LRU_C = 8.0
D_FFN = 2816
D_IN = 4 * D_RET + D_SSM + 2 * D_LRU
D_SCAN = D_SSM + 2 * D_LRU
EPS = 1e-6
GN_EPS = 1e-5

LANES = 128
SEQ_TILE = 8
ROW_BLOCK = 512
RET_BLOCK = 256
RET_ROWS = 1024
PAST_LEN = 1024
SCAN_BLOCK = 64
VMEM_LIMIT = 56 * 1024 * 1024


def _const_spec(shape):
    nd = len(shape)
    return pl.BlockSpec(shape, lambda *_: (0,) * nd, pipeline_mode=pl.Buffered(1))


def _layer_spec(shape, layer):
    nd = len(shape)
    return pl.BlockSpec((None,) + tuple(shape), lambda *_: (layer,) + (0,) * nd, pipeline_mode=pl.Buffered(1))


def _sigmoid(x):
    return 0.5 * (1.0 + jnp.tanh(0.5 * x))


def _gelu_tanh(x):
    return 0.5 * x * (1.0 + jnp.tanh(math.sqrt(2.0 / math.pi) * (x + 0.044715 * (x * x * x))))


def _rms_rows(x, eps):
    return x * lax.rsqrt(jnp.mean(x * x, axis=-1, keepdims=True) + eps)


def _proj_kernel(x_ref, nw_ref, w_ref, rot_ref, qkv_ref, g_ref, sc_ref):
    x = x_ref[...]
    h = (_rms_rows(x, EPS) * nw_ref[...]).astype(BF16)
    proj = jnp.dot(h, w_ref[...], preferred_element_type=F32)
    cos2 = rot_ref[:, :HEAD_DIM]
    sin2 = rot_ref[:, HEAD_DIM:]
    q_scale = HEAD_DIM ** -0.5
    for head in range(2 * RET_HEADS):
        lo = head * HEAD_DIM
        xh = proj[:, lo:lo + HEAD_DIM]
        r = xh * cos2 + pltpu.roll(xh, HEAD_DIM // 2, axis=1) * sin2
        if head < RET_HEADS:
            r = r * q_scale
        qkv_ref[:, lo:lo + HEAD_DIM] = r.astype(BF16)
    qkv_ref[:, 2 * D_RET:3 * D_RET] = proj[:, 2 * D_RET:3 * D_RET].astype(BF16)
    gg = proj[:, 3 * D_RET:4 * D_RET]
    g_ref[...] = gg * _sigmoid(gg)
    sc_ref[:, :D_SSM + D_LRU] = proj[:, 4 * D_RET:4 * D_RET + D_SSM + D_LRU]
    sc_ref[:, D_SSM + D_LRU:] = _gelu_tanh(proj[:, 4 * D_RET + D_SSM + D_LRU:])


def _proj_call(x2d, nw, w_in, rot, m, nper, layer):
    n = x2d.shape[0]
    return pl.pallas_call(
        _proj_kernel,
        grid=(n // m,),
        in_specs=[
            pl.BlockSpec((m, D_MODEL), lambda i: (i, 0)),
            _layer_spec((1, D_MODEL), layer),
            _layer_spec((D_MODEL, D_IN), layer),
            pl.BlockSpec((m, 2 * HEAD_DIM), lambda i: (i % nper, 0)),
        ],
        out_specs=[
            pl.BlockSpec((m, 3 * D_RET), lambda i: (i, 0)),
            pl.BlockSpec((m, D_RET), lambda i: (i, 0)),
            pl.BlockSpec((m, D_SCAN), lambda i: (i, 0)),
        ],
        out_shape=[
            jax.ShapeDtypeStruct((n, 3 * D_RET), BF16),
            jax.ShapeDtypeStruct((n, D_RET), F32),
            jax.ShapeDtypeStruct((n, D_SCAN), F32),
        ],
        compiler_params=pltpu.CompilerParams(
            dimension_semantics=("arbitrary",), vmem_limit_bytes=VMEM_LIMIT),
        name="proj",
    )(x2d, nw, w_in, rot)


def _ret_kernel(qkv_ref, g_ref, s0_ref, s_all_ref, dmask_ref, qdec_ref, kdec_ref, gblk_ref,
                o_ref, sfin_ref, s_scr, *, nseq):
    del s_all_ref
    c = pl.program_id(1)

    @pl.when(c == 0)
    def _():
        s_scr[...] = s0_ref[...]

    for j in range(nseq):
        for h in range(RET_HEADS):
            lo = h * HEAD_DIM
            q = qkv_ref[j, :, lo:lo + HEAD_DIM]
            k = qkv_ref[j, :, D_RET + lo:D_RET + lo + HEAD_DIM]
            v = qkv_ref[j, :, 2 * D_RET + lo:2 * D_RET + lo + HEAD_DIM]
            s = s_scr[j, h]
            scores = lax.dot_general(q, k, (((1,), (1,)), ((), ())), preferred_element_type=F32)
            a = (scores * dmask_ref[h]).astype(BF16)
            o = jnp.dot(a, v, preferred_element_type=F32)
            o = o + jnp.dot(q, s.astype(BF16), preferred_element_type=F32) * qdec_ref[h]
            kd = (k.astype(F32) * kdec_ref[h]).astype(BF16)
            u = lax.dot_general(kd, v, (((0,), (0,)), ((), ())), preferred_element_type=F32)
            s_scr[j, h] = gblk_ref[h] * s + u
            oc = o - jnp.mean(o, axis=-1, keepdims=True)
            on = oc * lax.rsqrt(jnp.mean(oc * oc, axis=-1, keepdims=True) + GN_EPS)
            o_ref[j, :, lo:lo + HEAD_DIM] = on * g_ref[j, :, lo:lo + HEAD_DIM]

    @pl.when(c == pl.num_programs(1) - 1)
    def _():
        sfin_ref[...] = s_scr[...]


def _ret_tables(tb):
    h = jnp.arange(RET_HEADS, dtype=F32)
    gamma = 1.0 - 2.0 ** (-5.0 - h)
    log_g = jnp.log(gamma)
    idx = jnp.arange(tb, dtype=F32)
    chunk = jnp.arange(tb, dtype=jnp.int32) // RET_CHUNK
    diff = idx[:, None] - idx[None, :]
    same = chunk[:, None] == chunk[None, :]
    earlier = chunk[:, None] > chunk[None, :]
    expo = jnp.where(same, jnp.abs(diff), diff)
    dmask = jnp.where((same | earlier)[None], jnp.exp(log_g[:, None, None] * expo[None]), 0.0)
    qdec = jnp.exp(log_g[:, None] * (idx + 1.0)[None, :])
    kdec = jnp.exp(log_g[:, None] * (tb - 1.0 - idx)[None, :])
    gblk = jnp.exp(log_g * tb)
    wide = lambda t: jnp.broadcast_to(t[:, :, None], t.shape + (HEAD_DIM,))
    return dmask, wide(qdec), wide(kdec), jnp.broadcast_to(gblk[:, None, None], (RET_HEADS, 1, HEAD_DIM))


def _ret_call(qkv, g, s0, s_all, tables, tb, nseq, layer):
    b, l, _ = qkv.shape
    dmask, qdec, kdec, gblk = tables
    state_block = (None, nseq, RET_HEADS, HEAD_DIM, HEAD_DIM)
    if s0.shape[0] == 1:
        s0_spec = pl.BlockSpec(state_block, lambda i, c: (0, 0, 0, 0, 0))
    else:
        s0_spec = pl.BlockSpec(state_block, lambda i, c: (layer, i, 0, 0, 0))
    return pl.pallas_call(
        functools.partial(_ret_kernel, nseq=nseq),
        grid=(b // nseq, l // tb),
        in_specs=[
            pl.BlockSpec((nseq, tb, 3 * D_RET), lambda i, c: (i, c, 0)),
            pl.BlockSpec((nseq, tb, D_RET), lambda i, c: (i, c, 0)),
            s0_spec,
            pl.BlockSpec(memory_space=pl.ANY),
            _const_spec((RET_HEADS, tb, tb)),
            _const_spec((RET_HEADS, tb, HEAD_DIM)),
            _const_spec((RET_HEADS, tb, HEAD_DIM)),
            _const_spec((RET_HEADS, 1, HEAD_DIM)),
        ],
        out_specs=[
            pl.BlockSpec((nseq, tb, D_RET), lambda i, c: (i, c, 0)),
            pl.BlockSpec(state_block, lambda i, c: (layer, i, 0, 0, 0)),
        ],
        out_shape=[
            jax.ShapeDtypeStruct((b, l, D_RET), F32),
            jax.ShapeDtypeStruct(s_all.shape, F32),
        ],
        input_output_aliases={3: 1},
        scratch_shapes=[pltpu.VMEM((nseq, RET_HEADS, HEAD_DIM, HEAD_DIM), F32)],
        compiler_params=pltpu.CompilerParams(
            dimension_semantics=("arbitrary", "arbitrary"), vmem_limit_bytes=VMEM_LIMIT),
        name="ret",
    )(qkv, g, s0, s_all, dmask, qdec, kdec, gblk)


def _scan_kernel(sc_ref, hs0_ref, hl0_ref, cc0_ref,
                 bmat_ref, lam_ref, cmat_ref, dskip_ref,
                 convw_ref, convb_ref, wa_ref, ba_ref, wx_ref, bxb_ref, sp_ref,
                 o_ref, hs_ref, hl_ref, cc_ref,
                 tm, hbuf, hst, xpad, abuf, bbuf, lst, res, *, tt):
    c = pl.program_id(1)
    rows = tt * SEQ_TILE
    pad_rows = (CONV_W - 1) * SEQ_TILE

    @pl.when(c == 0)
    def _():
        hst[...] = hs0_ref[...]
        lst[...] = hl0_ref[...]
        for j in range(CONV_W - 1):
            xpad[j * SEQ_TILE:(j + 1) * SEQ_TILE, :] = cc0_ref[j]

    for s in range(SEQ_TILE):
        for j in range(D_SCAN // LANES):
            tm[j, pl.ds(s, tt, stride=SEQ_TILE), :] = sc_ref[s, :, j * LANES:(j + 1) * LANES]

    u = jnp.concatenate([tm[0], tm[1]], axis=-1)
    hbuf[...] = jnp.dot(u.astype(BF16), bmat_ref[...], preferred_element_type=F32)

    xpad[pad_rows:, :] = jnp.concatenate([tm[2], tm[3]], axis=-1)
    xc = convb_ref[...] + xpad[0:rows, :] * convw_ref[0:1, :]
    for j in range(1, CONV_W):
        xc = xc + xpad[j * SEQ_TILE:j * SEQ_TILE + rows, :] * convw_ref[j:j + 1, :]
    xcb = xc.astype(BF16)
    r = _sigmoid(jnp.dot(xcb, wa_ref[...], preferred_element_type=F32) + ba_ref[...])
    ig = _sigmoid(jnp.dot(xcb, wx_ref[...], preferred_element_type=F32) + bxb_ref[...])
    log_a = (-LRU_C) * r * sp_ref[...]
    a = jnp.exp(log_a)
    abuf[...] = a
    t = 1.0 - a * a
    bbuf[...] = jnp.where(t > 0.0, t * lax.rsqrt(t), 0.0) * (ig * xc)

    lam_re = lam_ref[0:1, :]
    lam_im = lam_ref[1:2, :]

    def step(t, carry):
        hr, hi, hl = carry
        off = pl.multiple_of(t * SEQ_TILE, SEQ_TILE)
        br = hbuf[pl.ds(off, SEQ_TILE), 0:N_SSM]
        bi = hbuf[pl.ds(off, SEQ_TILE), N_SSM:2 * N_SSM]
        nr = lam_re * hr - lam_im * hi + br
        ni = lam_re * hi + lam_im * hr + bi
        hbuf[pl.ds(off, SEQ_TILE), 0:N_SSM] = nr
        hbuf[pl.ds(off, SEQ_TILE), N_SSM:2 * N_SSM] = ni
        nl = abuf[pl.ds(off, SEQ_TILE), :] * hl + bbuf[pl.ds(off, SEQ_TILE), :]
        bbuf[pl.ds(off, SEQ_TILE), :] = nl
        return nr, ni, nl

    hr, hi, hl = lax.fori_loop(0, tt, step, (hst[:, 0:N_SSM], hst[:, N_SSM:2 * N_SSM], lst[...]))
    hst[:, 0:N_SSM] = hr
    hst[:, N_SSM:2 * N_SSM] = hi
    lst[...] = hl
    for j in range(CONV_W - 1):
        xpad[j * SEQ_TILE:(j + 1) * SEQ_TILE, :] = xpad[rows + j * SEQ_TILE:rows + (j + 1) * SEQ_TILE, :]

    y = jnp.dot(hbuf[...].astype(BF16), cmat_ref[...], preferred_element_type=F32) + dskip_ref[...] * u
    o_lru = bbuf[...] * jnp.concatenate([tm[4], tm[5]], axis=-1)
    res[0] = y[:, :LANES]
    res[1] = y[:, LANES:]
    res[2] = o_lru[:, :LANES]
    res[3] = o_lru[:, LANES:]

    for s in range(SEQ_TILE):
        for j in range((D_SSM + D_LRU) // LANES):
            o_ref[s, :, j * LANES:(j + 1) * LANES] = res[j, pl.ds(s, tt, stride=SEQ_TILE), :]

    @pl.when(c == pl.num_programs(1) - 1)
    def _():
        hs_ref[...] = hst[...]
        hl_ref[...] = lst[...]
        for j in range(CONV_W - 1):
            cc_ref[j] = xpad[j * SEQ_TILE:(j + 1) * SEQ_TILE, :]


def _scan_call(sc, hs0, hl0, cc0, sp, tt):
    b, l, _ = sc.shape
    rows = tt * SEQ_TILE
    seq_spec = lambda w: pl.BlockSpec((SEQ_TILE, w), lambda i, c: (i, 0))
    cc_spec = pl.BlockSpec((CONV_W - 1, SEQ_TILE, D_LRU), lambda i, c: (0, i, 0))
    return pl.pallas_call(
        functools.partial(_scan_kernel, tt=tt),
        grid=(b // SEQ_TILE, l // tt),
        in_specs=[
            pl.BlockSpec((SEQ_TILE, tt, D_SCAN), lambda i, c: (i, c, 0)),
            seq_spec(2 * N_SSM), seq_spec(D_LRU), cc_spec,
            _const_spec((D_SSM, 2 * N_SSM)), _const_spec((2, N_SSM)), _const_spec((2 * N_SSM, D_SSM)),
            _const_spec((1, D_SSM)),
            _const_spec((CONV_W, D_LRU)), _const_spec((1, D_LRU)),
            _const_spec((D_LRU, D_LRU)), _const_spec((1, D_LRU)),
            _const_spec((D_LRU, D_LRU)), _const_spec((1, D_LRU)), _const_spec((1, D_LRU)),
        ],
        out_specs=[
            pl.BlockSpec((SEQ_TILE, tt, D_SSM + D_LRU), lambda i, c: (i, c, 0)),
            seq_spec(2 * N_SSM), seq_spec(D_LRU), cc_spec,
        ],
        out_shape=[
            jax.ShapeDtypeStruct((b, l, D_SSM + D_LRU), F32),
            jax.ShapeDtypeStruct((b, 2 * N_SSM), F32),
            jax.ShapeDtypeStruct((b, D_LRU), F32),
            jax.ShapeDtypeStruct((CONV_W - 1, b, D_LRU), F32),
        ],
        scratch_shapes=[
            pltpu.VMEM((D_SCAN // LANES, rows, LANES), F32),
            pltpu.VMEM((rows, 2 * N_SSM), F32),
            pltpu.VMEM((SEQ_TILE, 2 * N_SSM), F32),
            pltpu.VMEM((rows + (CONV_W - 1) * SEQ_TILE, D_LRU), F32),
            pltpu.VMEM((rows, D_LRU), F32),
            pltpu.VMEM((rows, D_LRU), F32),
            pltpu.VMEM((SEQ_TILE, D_LRU), F32),
            pltpu.VMEM(((D_SSM + D_LRU) // LANES, rows, LANES), F32),
        ],
        compiler_params=pltpu.CompilerParams(
            dimension_semantics=("arbitrary", "arbitrary"), vmem_limit_bytes=VMEM_LIMIT),
        name="scan",
    )(sc, hs0, hl0, cc0, *sp)


def _scan_params(a_re, a_im, b_re, b_im, c_re, c_im, d_skip, log_dt,
                 conv_w, conv_b, w_a, b_a, w_x, b_x, lam):
    dt = jnp.exp(log_dt)[:, None]
    mag = jnp.exp(a_re * dt)
    lb_re = mag * jnp.cos(a_im * dt)
    lb_im = mag * jnp.sin(a_im * dt)
    den = a_re * a_re + a_im * a_im
    f_re = ((lb_re - 1.0) * a_re + lb_im * a_im) / den
    f_im = (lb_im * a_re - (lb_re - 1.0) * a_im) / den
    bb_re = f_re[..., None] * b_re - f_im[..., None] * b_im
    bb_im = f_re[..., None] * b_im + f_im[..., None] * b_re
    eye_g = jnp.eye(SSM_GROUPS, dtype=F32)
    bd_in = lambda t: jnp.einsum('gpc,gh->gchp', t, eye_g).reshape(D_SSM, N_SSM)
    bmat = jnp.concatenate([bd_in(bb_re), bd_in(bb_im)], axis=1).astype(BF16)
    bd_out = lambda t: jnp.einsum('gcp,gh->gphc', t, eye_g).reshape(N_SSM, D_SSM)
    cmat = jnp.concatenate([bd_out(c_re), bd_out(-c_im)], axis=0).astype(BF16)
    lam_bar = jnp.stack([lb_re.reshape(N_SSM), lb_im.reshape(N_SSM)])
    eye_b = jnp.eye(LRU_BLOCKS, dtype=F32)
    bd_lru = lambda t: jnp.einsum('hij,hk->hikj', t, eye_b).reshape(D_LRU, D_LRU).astype(BF16)
    row = lambda t: t.reshape(1, -1)
    return (bmat, lam_bar, cmat, row(d_skip),
            conv_w, row(conv_b), bd_lru(w_a), row(b_a), bd_lru(w_x), row(b_x),
            row(jax.nn.softplus(-lam)))


def _merge_kernel(x_ref, oret_ref, osl_ref, wglu_ref, ms_ref, wo_ref, nf_ref, wg_ref, wu_ref, wd_ref, nfin_ref,
                  y_ref, *, final):
    x = x_ref[...]
    osl = osl_ref[...]
    ga = jnp.dot(_gelu_tanh(osl[:, :D_SSM]).astype(BF16), wglu_ref[...], preferred_element_type=F32)
    o_ssm = ga[:, :D_SSM] * _sigmoid(ga[:, D_SSM:])
    mixed = jnp.concatenate(
        [_rms_rows(oret_ref[...], EPS), _rms_rows(o_ssm, EPS), _rms_rows(osl[:, D_SSM:], EPS)],
        axis=-1) * ms_ref[...]
    x = x + jnp.dot(mixed.astype(BF16), wo_ref[...], preferred_element_type=F32)
    h = (_rms_rows(x, EPS) * nf_ref[...]).astype(BF16)
    gate = jnp.dot(h, wg_ref[...], preferred_element_type=F32)
    up = jnp.dot(h, wu_ref[...], preferred_element_type=F32)
    ff = (gate * _sigmoid(gate) * up).astype(BF16)
    x = x + jnp.dot(ff, wd_ref[...], preferred_element_type=F32)
    if final:
        x = _rms_rows(x, EPS) * nfin_ref[...]
    y_ref[...] = x


def _merge_call(x2d, oret, osl, wglu, ms, wo, nf, wg, wu, wd, nfin, m, layer, final):
    n = x2d.shape[0]
    row_spec = lambda w: pl.BlockSpec((m, w), lambda i: (i, 0))
    return pl.pallas_call(
        functools.partial(_merge_kernel, final=final),
        grid=(n // m,),
        in_specs=[
            row_spec(D_MODEL), row_spec(D_RET), row_spec(D_SSM + D_LRU),
            _layer_spec((D_SSM, 2 * D_SSM), layer),
            _layer_spec((1, D_MODEL), layer), _layer_spec((D_MODEL, D_MODEL), layer),
            _layer_spec((1, D_MODEL), layer),
            _layer_spec((D_MODEL, D_FFN), layer), _layer_spec((D_MODEL, D_FFN), layer),
            _layer_spec((D_FFN, D_MODEL), layer),
            _const_spec((1, D_MODEL)),
        ],
        out_specs=row_spec(D_MODEL),
        out_shape=jax.ShapeDtypeStruct((n, D_MODEL), F32),
        compiler_params=pltpu.CompilerParams(
            dimension_semantics=("arbitrary",), vmem_limit_bytes=VMEM_LIMIT),
        name="merge",
    )(x2d, oret, osl, wglu, ms, wo, nf, wg, wu, wd, nfin)


def _rotary_table(pos_offset, l, m):
    pos = pos_offset + jnp.arange(l, dtype=jnp.int32)
    inv_freq = ROPE_BASE ** (-jnp.arange(0, HEAD_DIM, 2, dtype=F32) / HEAD_DIM)
    ang = pos.astype(F32)[:, None] * inv_freq[None, :]
    cos, sin = jnp.cos(ang), jnp.sin(ang)
    tab = jnp.concatenate([cos, cos, -sin, sin], axis=-1)
    if l < m:
        tab = jnp.tile(tab, (m // l, 1))
    return tab


def _trunk(x, pos_offset, s_ret, s_ssm, s_lru, c_conv, w, scan_params):
    b, l, _ = x.shape
    n = b * l
    m = min(ROW_BLOCK, n)
    tb = min(RET_BLOCK, l)
    tt = min(SCAN_BLOCK, l)
    nseq = min(b, max(1, RET_ROWS // tb))
    assert n % m == 0 and (l % m == 0 or m % l == 0) and l % tb == 0 and l % tt == 0
    assert b % SEQ_TILE == 0 and b % nseq == 0
    rot = _rotary_table(pos_offset, l, m)
    nper = max(l // m, 1)
    tables = _ret_tables(tb)
    depth = len(scan_params)
    x2d = x.reshape(n, D_MODEL)
    new_ssm, new_lru, new_conv = [], [], []
    if s_ret is None:
        s_ret = jnp.zeros((1, nseq, RET_HEADS, HEAD_DIM, HEAD_DIM), F32)
    ret_all = jnp.zeros((depth, b, RET_HEADS, HEAD_DIM, HEAD_DIM), F32)
    for i in range(depth):
        qkv, g, sc = _proj_call(x2d, w['norm_mix'], w['w_in'], rot, m, nper, i)
        oret, ret_all = _ret_call(qkv.reshape(b, l, 3 * D_RET), g.reshape(b, l, D_RET), s_ret, ret_all,
                                  tables, tb, nseq, i)
        hs0 = jnp.concatenate([s_ssm[i][..., 0].reshape(b, N_SSM), s_ssm[i][..., 1].reshape(b, N_SSM)], axis=1)
        osl, hs, hl, cc = _scan_call(sc.reshape(b, l, D_SCAN), hs0, s_lru[i],
                                     jnp.swapaxes(c_conv[i], 0, 1), scan_params[i], tt)
        x2d = _merge_call(x2d, oret.reshape(n, D_RET), osl.reshape(n, D_SSM + D_LRU),
                          w['w_glu'], w['mix_scale'], w['w_out'], w['norm_ffn'], w['w_gate'], w['w_up'], w['w_down'],
                          w['norm_final'], m, i, final=(i == depth - 1))
        new_ssm.append(jnp.stack([hs[:, :N_SSM].reshape(b, SSM_GROUPS, SSM_STATE),
                                  hs[:, N_SSM:].reshape(b, SSM_GROUPS, SSM_STATE)], axis=-1))
        new_lru.append(hl)
        new_conv.append(jnp.swapaxes(cc, 0, 1))
    return (x2d.reshape(b, l, D_MODEL), ret_all, jnp.stack(new_ssm),
            jnp.stack(new_lru), jnp.stack(new_conv))


def _prepare_weights(norm_mix, w_in, mix_scale, w_out, ssm_w_glu, norm_ffn, w_ffn_gate, w_ffn_up, w_ffn_down,
                     norm_final):
    depth = norm_mix.shape[0]
    rows = lambda t: t.reshape(depth, 1, D_MODEL)
    return {
        'norm_mix': rows(norm_mix), 'w_in': w_in.astype(BF16),
        'mix_scale': rows(mix_scale), 'w_out': w_out.astype(BF16), 'w_glu': ssm_w_glu.astype(BF16),
        'norm_ffn': rows(norm_ffn), 'w_gate': w_ffn_gate.astype(BF16), 'w_up': w_ffn_up.astype(BF16),
        'w_down': w_ffn_down.astype(BF16), 'norm_final': norm_final.reshape(1, D_MODEL),
    }


def kernel(x_prompt, x_sample, state_ret, state_ssm, state_lru, cache_conv, norm_mix, w_in, mix_scale, w_out, ssm_a_re, ssm_a_im, ssm_b_re, ssm_b_im, ssm_c_re, ssm_c_im, ssm_d, ssm_log_dt, ssm_w_glu, lru_conv_w, lru_conv_b, lru_w_a, lru_b_a, lru_w_x, lru_b_x, lru_lambda, norm_ffn, w_ffn_gate, w_ffn_up, w_ffn_down, norm_final):
    w = _prepare_weights(norm_mix, w_in, mix_scale, w_out, ssm_w_glu, norm_ffn, w_ffn_gate, w_ffn_up, w_ffn_down,
                         norm_final)
    depth = norm_mix.shape[0]
    scan_params = [
        _scan_params(ssm_a_re[i], ssm_a_im[i], ssm_b_re[i], ssm_b_im[i], ssm_c_re[i], ssm_c_im[i],
                     ssm_d[i], ssm_log_dt[i], lru_conv_w[i], lru_conv_b[i],
                     lru_w_a[i], lru_b_a[i], lru_w_x[i], lru_b_x[i], lru_lambda[i])
        for i in range(depth)]
    bp = x_prompt.shape[0]
    z_ssm = jnp.zeros((depth, bp, SSM_GROUPS, SSM_STATE, 2), F32)
    z_lru = jnp.zeros((depth, bp, D_LRU), F32)
    z_conv = jnp.zeros((depth, bp, CONV_W - 1, D_LRU), F32)
    out_p = _trunk(x_prompt, 0, None, z_ssm, z_lru, z_conv, w, scan_params)
    out_s = _trunk(x_sample, PAST_LEN, state_ret, state_ssm, state_lru, cache_conv, w, scan_params)
    return (out_p[0], out_s[0]) + tuple(out_p[1:]) + tuple(out_s[1:])
```

```python
import functools
import math

import jax
import jax.numpy as jnp
from jax import lax
from jax.experimental import pallas as pl
from jax.experimental.pallas import tpu as pltpu

F32 = jnp.float32
BF16 = jnp.bfloat16

D_MODEL = 1024
D_RET = 512
D_SSM = 256
D_LRU = 256
RET_HEADS = 4
HEAD_DIM = 128
RET_CHUNK = 64
ROPE_BASE = 10000.0
SSM_GROUP = 16
SSM_GROUPS = 16
SSM_STATE = 64
N_SSM = SSM_GROUPS * SSM_STATE
LRU_BLOCKS = 4
CONV_W = 4
LRU_C = 8.0
D_FFN = 2816
D_IN = 4 * D_RET + D_SSM + 2 * D_LRU
D_SCAN = D_SSM + 2 * D_LRU
EPS = 1e-6
GN_EPS = 1e-5

LANES = 128
SEQ_TILE = 8
ROW_BLOCK = 512
RET_BLOCK = 256
RET_ROWS = 1024
PAST_LEN = 1024
SCAN_BLOCK = 64
FFN_SPLITS = ((0, 1536), (1536, D_FFN))
VMEM_LIMIT = 60 * 1024 * 1024


def _const_spec(shape):
    nd = len(shape)
    return pl.BlockSpec(shape, lambda *_: (0,) * nd, pipeline_mode=pl.Buffered(1))


def _layer_spec(shape, layer):
    nd = len(shape)
    return pl.BlockSpec((None,) + tuple(shape), lambda *_: (layer,) + (0,) * nd, pipeline_mode=pl.Buffered(1))


def _sigmoid(x):
    return 0.5 * (1.0 + jnp.tanh(0.5 * x))


def _gelu_tanh(x):
    return 0.5 * x * (1.0 + jnp.tanh(math.sqrt(2.0 / math.pi) * (x + 0.044715 * (x * x * x))))


def _rms_rows(x, eps):
    return x * lax.rsqrt(jnp.mean(x * x, axis=-1, keepdims=True) + eps)


def _proj_kernel(x_ref, nw_ref, w_ref, rot_ref, qkv_ref, g_ref, sc_ref):
    x = x_ref[...]
    h = (_rms_rows(x, EPS) * nw_ref[...]).astype(BF16)
    proj = jnp.dot(h, w_ref[...], preferred_element_type=F32)
    cos2 = rot_ref[:, :HEAD_DIM]
    sin2 = rot_ref[:, HEAD_DIM:]
    q_scale = HEAD_DIM ** -0.5
    for head in range(2 * RET_HEADS):
        lo = head * HEAD_DIM
        xh = proj[:, lo:lo + HEAD_DIM]
        r = xh * cos2 + pltpu.roll(xh, HEAD_DIM // 2, axis=1) * sin2
        if head < RET_HEADS:
            r = r * q_scale
        qkv_ref[:, lo:lo + HEAD_DIM] = r.astype(BF16)
    qkv_ref[:, 2 * D_RET:3 * D_RET] = proj[:, 2 * D_RET:3 * D_RET].astype(BF16)
    gg = proj[:, 3 * D_RET:4 * D_RET]
    g_ref[...] = gg * _sigmoid(gg)
    sc_ref[:, :D_SSM + D_LRU] = proj[:, 4 * D_RET:4 * D_RET + D_SSM + D_LRU]
    sc_ref[:, D_SSM + D_LRU:] = _gelu_tanh(proj[:, 4 * D_RET + D_SSM + D_LRU:])


def _proj_call(x2d, nw, w_in, rot, m, nper, layer):
    n = x2d.shape[0]
    return pl.pallas_call(
        _proj_kernel,
        grid=(n // m,),
        in_specs=[
            pl.BlockSpec((m, D_MODEL), lambda i: (i, 0)),
            _layer_spec((1, D_MODEL), layer),
            _layer_spec((D_MODEL, D_IN), layer),
            pl.BlockSpec((m, 2 * HEAD_DIM), lambda i: (i % nper, 0)),
        ],
        out_specs=[
            pl.BlockSpec((m, 3 * D_RET), lambda i: (i, 0)),
            pl.BlockSpec((m, D_RET), lambda i: (i, 0)),
            pl.BlockSpec((m, D_SCAN), lambda i: (i, 0)),
        ],
        out_shape=[
            jax.ShapeDtypeStruct((n, 3 * D_RET), BF16),
            jax.ShapeDtypeStruct((n, D_RET), F32),
            jax.ShapeDtypeStruct((n, D_SCAN), F32),
        ],
        compiler_params=pltpu.CompilerParams(
            dimension_semantics=("arbitrary",), vmem_limit_bytes=VMEM_LIMIT),
        name="proj",
    )(x2d, nw, w_in, rot)


def _ret_kernel(qkv_ref, g_ref, s0_ref, s_all_ref, dmask_ref, qdec_ref, kdec_ref, gblk_ref,
                o_ref, sfin_ref, s_scr, *, nseq):
    del s_all_ref
    c = pl.program_id(1)

    @pl.when(c == 0)
    def _():
        s_scr[...] = s0_ref[...]

    for j in range(nseq):
        for h in range(RET_HEADS):
            lo = h * HEAD_DIM
            q = qkv_ref[j, :, lo:lo + HEAD_DIM]
            k = qkv_ref[j, :, D_RET + lo:D_RET + lo + HEAD_DIM]
            v = qkv_ref[j, :, 2 * D_RET + lo:2 * D_RET + lo + HEAD_DIM]
            s = s_scr[j, h]
            scores = lax.dot_general(q, k, (((1,), (1,)), ((), ())), preferred_element_type=F32)
            a = (scores * dmask_ref[h]).astype(BF16)
            o = jnp.dot(a, v, preferred_element_type=F32)
            o = o + jnp.dot(q, s.astype(BF16), preferred_element_type=F32) * qdec_ref[h]
            kd = (k.astype(F32) * kdec_ref[h]).astype(BF16)
            u = lax.dot_general(kd, v, (((0,), (0,)), ((), ())), preferred_element_type=F32)
            s_scr[j, h] = gblk_ref[h] * s + u
            oc = o - jnp.mean(o, axis=-1, keepdims=True)
            on = oc * lax.rsqrt(jnp.mean(oc * oc, axis=-1, keepdims=True) + GN_EPS)
            o_ref[j, :, lo:lo + HEAD_DIM] = on * g_ref[j, :, lo:lo + HEAD_DIM]

    @pl.when(c == pl.num_programs(1) - 1)
    def _():
        sfin_ref[...] = s_scr[...]


def _ret_tables(tb):
    h = jnp.arange(RET_HEADS, dtype=F32)
    gamma = 1.0 - 2.0 ** (-5.0 - h)
    log_g = jnp.log(gamma)
    idx = jnp.arange(tb, dtype=F32)
    chunk = jnp.arange(tb, dtype=jnp.int32) // RET_CHUNK
    diff = idx[:, None] - idx[None, :]
    same = chunk[:, None] == chunk[None, :]
    earlier = chunk[:, None] > chunk[None, :]
    expo = jnp.where(same, jnp.abs(diff), diff)
    dmask = jnp.where((same | earlier)[None], jnp.exp(log_g[:, None, None] * expo[None]), 0.0)
    qdec = jnp.exp(log_g[:, None] * (idx + 1.0)[None, :])
    kdec = jnp.exp(log_g[:, None] * (tb - 1.0 - idx)[None, :])
    gblk = jnp.exp(log_g * tb)
    wide = lambda t: jnp.broadcast_to(t[:, :, None], t.shape + (HEAD_DIM,))
    return dmask, wide(qdec), wide(kdec), jnp.broadcast_to(gblk[:, None, None], (RET_HEADS, 1, HEAD_DIM))


def _ret_call(qkv, g, s0, s_all, tables, tb, nseq, layer):
    b, l, _ = qkv.shape
    dmask, qdec, kdec, gblk = tables
    state_block = (None, nseq, RET_HEADS, HEAD_DIM, HEAD_DIM)
    if s0.shape[0] == 1:
        s0_spec = pl.BlockSpec(state_block, lambda i, c: (0, 0, 0, 0, 0))
    else:
        s0_spec = pl.BlockSpec(state_block, lambda i, c: (layer, i, 0, 0, 0))
    return pl.pallas_call(
        functools.partial(_ret_kernel, nseq=nseq),
        grid=(b // nseq, l // tb),
        in_specs=[
            pl.BlockSpec((nseq, tb, 3 * D_RET), lambda i, c: (i, c, 0)),
            pl.BlockSpec((nseq, tb, D_RET), lambda i, c: (i, c, 0)),
            s0_spec,
            pl.BlockSpec(memory_space=pl.ANY),
            _const_spec((RET_HEADS, tb, tb)),
            _const_spec((RET_HEADS, tb, HEAD_DIM)),
            _const_spec((RET_HEADS, tb, HEAD_DIM)),
            _const_spec((RET_HEADS, 1, HEAD_DIM)),
        ],
        out_specs=[
            pl.BlockSpec((nseq, tb, D_RET), lambda i, c: (i, c, 0)),
            pl.BlockSpec(state_block, lambda i, c: (layer, i, 0, 0, 0)),
        ],
        out_shape=[
            jax.ShapeDtypeStruct((b, l, D_RET), F32),
            jax.ShapeDtypeStruct(s_all.shape, F32),
        ],
        input_output_aliases={3: 1},
        scratch_shapes=[pltpu.VMEM((nseq, RET_HEADS, HEAD_DIM, HEAD_DIM), F32)],
        compiler_params=pltpu.CompilerParams(
            dimension_semantics=("arbitrary", "arbitrary"), vmem_limit_bytes=VMEM_LIMIT),
        name="ret",
    )(qkv, g, s0, s_all, dmask, qdec, kdec, gblk)


def _scan_block(sc_ref, params, scratch, osl_out, tt):
    bmat_ref, lam_ref, cmat_ref, dskip_ref, convw_ref, convb_ref, wa_ref, ba_ref, wx_ref, bxb_ref, sp_ref = params
    tm, hbuf, hst, xpad, abuf, bbuf, lst, res = scratch
    rows = tt * SEQ_TILE
    pad_rows = (CONV_W - 1) * SEQ_TILE

    for s in range(SEQ_TILE):
        for j in range(D_SCAN // LANES):
            tm[j, pl.ds(s, tt, stride=SEQ_TILE), :] = sc_ref[s, :, j * LANES:(j + 1) * LANES]

    u = jnp.concatenate([tm[0], tm[1]], axis=-1)
    hbuf[...] = jnp.dot(u.astype(BF16), bmat_ref[...], preferred_element_type=F32)

    xpad[pad_rows:, :] = jnp.concatenate([tm[2], tm[3]], axis=-1)
    xc = convb_ref[...] + xpad[0:rows, :] * convw_ref[0:1, :]
    for j in range(1, CONV_W):
        xc = xc + xpad[j * SEQ_TILE:j * SEQ_TILE + rows, :] * convw_ref[j:j + 1, :]
    xcb = xc.astype(BF16)
    r = _sigmoid(jnp.dot(xcb, wa_ref[...], preferred_element_type=F32) + ba_ref[...])
    ig = _sigmoid(jnp.dot(xcb, wx_ref[...], preferred_element_type=F32) + bxb_ref[...])
    log_a = (-LRU_C) * r * sp_ref[...]
    a = jnp.exp(log_a)
    abuf[...] = a
    t1 = 1.0 - a * a
    bbuf[...] = jnp.where(t1 > 0.0, t1 * lax.rsqrt(t1), 0.0) * (ig * xc)

    lam_re = lam_ref[0:1, :]
    lam_im = lam_ref[1:2, :]
    hr = hst[:, 0:N_SSM]
    hi = hst[:, N_SSM:2 * N_SSM]
    hl = lst[...]
    for t in range(tt):
        lo = t * SEQ_TILE
        br = hbuf[lo:lo + SEQ_TILE, 0:N_SSM]
        bi = hbuf[lo:lo + SEQ_TILE, N_SSM:2 * N_SSM]
        hr, hi = lam_re * hr - lam_im * hi + br, lam_re * hi + lam_im * hr + bi
        hbuf[lo:lo + SEQ_TILE, 0:N_SSM] = hr
        hbuf[lo:lo + SEQ_TILE, N_SSM:2 * N_SSM] = hi
        hl = abuf[lo:lo + SEQ_TILE, :] * hl + bbuf[lo:lo + SEQ_TILE, :]
        bbuf[lo:lo + SEQ_TILE, :] = hl
    hst[:, 0:N_SSM] = hr
    hst[:, N_SSM:2 * N_SSM] = hi
    lst[...] = hl
    for j in range(CONV_W - 1):
        xpad[j * SEQ_TILE:(j + 1) * SEQ_TILE, :] = xpad[rows + j * SEQ_TILE:rows + (j + 1) * SEQ_TILE, :]

    y = jnp.dot(hbuf[...].astype(BF16), cmat_ref[...], preferred_element_type=F32) + dskip_ref[...] * u
    o_lru = bbuf[...] * jnp.concatenate([tm[4], tm[5]], axis=-1)
    res[0] = y[:, :LANES]
    res[1] = y[:, LANES:]
    res[2] = o_lru[:, :LANES]
    res[3] = o_lru[:, LANES:]
    for s in range(SEQ_TILE):
        for j in range((D_SSM + D_LRU) // LANES):
            osl_out[s * tt:(s + 1) * tt, j * LANES:(j + 1) * LANES] = res[j, pl.ds(s, tt, stride=SEQ_TILE), :]


def _merge_block(x, oret, osl, weights, final):
    wglu_ref, ms_ref, wo_ref, nf_ref, wg_ref, wu_ref, wd_ref, nfin_ref = weights
    ga = jnp.dot(_gelu_tanh(osl[:, :D_SSM]).astype(BF16), wglu_ref[...], preferred_element_type=F32)
    o_ssm = ga[:, :D_SSM] * _sigmoid(ga[:, D_SSM:])
    mixed = jnp.concatenate(
        [_rms_rows(oret, EPS), _rms_rows(o_ssm, EPS), _rms_rows(osl[:, D_SSM:], EPS)], axis=-1) * ms_ref[...]
    x = x + jnp.dot(mixed.astype(BF16), wo_ref[...], preferred_element_type=F32)
    h = (_rms_rows(x, EPS) * nf_ref[...]).astype(BF16)
    for lo, hi in FFN_SPLITS:
        gate = jnp.dot(h, wg_ref[:, lo:hi], preferred_element_type=F32)
        up = jnp.dot(h, wu_ref[:, lo:hi], preferred_element_type=F32)
        ff = (gate * _sigmoid(gate) * up).astype(BF16)
        x = x + jnp.dot(ff, wd_ref[lo:hi, :], preferred_element_type=F32)
    if final:
        x = _rms_rows(x, EPS) * nfin_ref[...]
    return x


def _mix_kernel(*refs, tt, nt, final):
    sc_ref, x_ref, oret_ref, hs0_ref, hl0_ref, cc0_ref = refs[:6]
    scan_params = refs[6:17]
    merge_weights = refs[17:25]
    y_ref, hs_ref, hl_ref, cc_ref = refs[25:29]
    scratch = refs[29:37]
    osl_cur, osl_prev = refs[37:39]
    tm, hbuf, hst, xpad, abuf, bbuf, lst, res = scratch
    c = pl.program_id(1)
    rows = tt * SEQ_TILE

    @pl.when(c == 0)
    def _():
        hst[...] = hs0_ref[...]
        lst[...] = hl0_ref[...]
        for j in range(CONV_W - 1):
            xpad[j * SEQ_TILE:(j + 1) * SEQ_TILE, :] = cc0_ref[j]
        osl_prev[...] = jnp.zeros(osl_prev.shape, F32)

    _scan_block(sc_ref, scan_params, scratch, osl_cur, tt)
    x = _merge_block(x_ref[...].reshape(rows, D_MODEL), oret_ref[...].reshape(rows, D_RET),
                     osl_prev[...], merge_weights, final)
    y_ref[...] = x.reshape(SEQ_TILE, tt, D_MODEL)
    osl_prev[...] = osl_cur[...]

    @pl.when(c == nt - 1)
    def _():
        hs_ref[...] = hst[...]
        hl_ref[...] = lst[...]
        for j in range(CONV_W - 1):
            cc_ref[j] = xpad[j * SEQ_TILE:(j + 1) * SEQ_TILE, :]


def _mix_call(sc, x, oret, hs0, hl0, cc0, sp, w, tt, layer, final):
    b, l, _ = sc.shape
    nt = l // tt
    rows = tt * SEQ_TILE
    cur = lambda width: pl.BlockSpec((SEQ_TILE, tt, width), lambda i, c: (i, jnp.minimum(c, nt - 1), 0))
    prev = lambda width: pl.BlockSpec((SEQ_TILE, tt, width), lambda i, c: (i, jnp.maximum(c - 1, 0), 0))
    seq_spec = lambda width: pl.BlockSpec((SEQ_TILE, width), lambda i, c: (i, 0))
    cc_spec = pl.BlockSpec((CONV_W - 1, SEQ_TILE, D_LRU), lambda i, c: (0, i, 0))
    return pl.pallas_call(
        functools.partial(_mix_kernel, tt=tt, nt=nt, final=final),
        grid=(b // SEQ_TILE, nt + 1),
        in_specs=[
            cur(D_SCAN), prev(D_MODEL), prev(D_RET),
            seq_spec(2 * N_SSM), seq_spec(D_LRU), cc_spec,
            _const_spec((D_SSM, 2 * N_SSM)), _const_spec((2, N_SSM)), _const_spec((2 * N_SSM, D_SSM)),
            _const_spec((1, D_SSM)),
            _const_spec((CONV_W, D_LRU)), _const_spec((1, D_LRU)),
            _const_spec((D_LRU, D_LRU)), _const_spec((1, D_LRU)),
            _const_spec((D_LRU, D_LRU)), _const_spec((1, D_LRU)), _const_spec((1, D_LRU)),
            _layer_spec((D_SSM, 2 * D_SSM), layer),
            _layer_spec((1, D_MODEL), layer), _layer_spec((D_MODEL, D_MODEL), layer),
            _layer_spec((1, D_MODEL), layer),
            _layer_spec((D_MODEL, D_FFN), layer), _layer_spec((D_MODEL, D_FFN), layer),
            _layer_spec((D_FFN, D_MODEL), layer),
            _const_spec((1, D_MODEL)),
        ],
        out_specs=[
            prev(D_MODEL),
            seq_spec(2 * N_SSM), seq_spec(D_LRU), cc_spec,
        ],
        out_shape=[
            jax.ShapeDtypeStruct((b, l, D_MODEL), F32),
            jax.ShapeDtypeStruct((b, 2 * N_SSM), F32),
            jax.ShapeDtypeStruct((b, D_LRU), F32),
            jax.ShapeDtypeStruct((CONV_W - 1, b, D_LRU), F32),
        ],
        scratch_shapes=[
            pltpu.VMEM((D_SCAN // LANES, rows, LANES), F32),
            pltpu.VMEM((rows, 2 * N_SSM), F32),
            pltpu.VMEM((SEQ_TILE, 2 * N_SSM), F32),
            pltpu.VMEM((rows + (CONV_W - 1) * SEQ_TILE, D_LRU), F32),
            pltpu.VMEM((rows, D_LRU), F32),
            pltpu.VMEM((rows, D_LRU), F32),
            pltpu.VMEM((SEQ_TILE, D_LRU), F32),
            pltpu.VMEM(((D_SSM + D_LRU) // LANES, rows, LANES), F32),
            pltpu.VMEM((rows, D_SSM + D_LRU), F32),
            pltpu.VMEM((rows, D_SSM + D_LRU), F32),
        ],
        compiler_params=pltpu.CompilerParams(
            dimension_semantics=("arbitrary", "arbitrary"), vmem_limit_bytes=VMEM_LIMIT),
        name="mix",
    )(sc, x, oret, hs0, hl0, cc0, *sp,
      w['w_glu'], w['mix_scale'], w['w_out'], w['norm_ffn'], w['w_gate'], w['w_up'], w['w_down'], w['norm_final'])


def _scan_params(a_re, a_im, b_re, b_im, c_re, c_im, d_skip, log_dt,
                 conv_w, conv_b, w_a, b_a, w_x, b_x, lam):
    dt = jnp.exp(log_dt)[:, None]
    mag = jnp.exp(a_re * dt)
    lb_re = mag * jnp.cos(a_im * dt)
    lb_im = mag * jnp.sin(a_im * dt)
    den = a_re * a_re + a_im * a_im
    f_re = ((lb_re - 1.0) * a_re + lb_im * a_im) / den
    f_im = (lb_im * a_re - (lb_re - 1.0) * a_im) / den
    bb_re = f_re[..., None] * b_re - f_im[..., None] * b_im
    bb_im = f_re[..., None] * b_im + f_im[..., None] * b_re
    eye_g = jnp.eye(SSM_GROUPS, dtype=F32)
    bd_in = lambda t: jnp.einsum('gpc,gh->gchp', t, eye_g).reshape(D_SSM, N_SSM)
    bmat = jnp.concatenate([bd_in(bb_re), bd_in(bb_im)], axis=1).astype(BF16)
    bd_out = lambda t: jnp.einsum('gcp,gh->gphc', t, eye_g).reshape(N_SSM, D_SSM)
    cmat = jnp.concatenate([bd_out(c_re), bd_out(-c_im)], axis=0).astype(BF16)
    lam_bar = jnp.stack([lb_re.reshape(N_SSM), lb_im.reshape(N_SSM)])
    eye_b = jnp.eye(LRU_BLOCKS, dtype=F32)
    bd_lru = lambda t: jnp.einsum('hij,hk->hikj', t, eye_b).reshape(D_LRU, D_LRU).astype(BF16)
    row = lambda t: t.reshape(1, -1)
    return (bmat, lam_bar, cmat, row(d_skip),
            conv_w, row(conv_b), bd_lru(w_a), row(b_a), bd_lru(w_x), row(b_x),
            row(jax.nn.softplus(-lam)))


def _rotary_table(pos_offset, l, m):
    pos = pos_offset + jnp.arange(l, dtype=jnp.int32)
    inv_freq = ROPE_BASE ** (-jnp.arange(0, HEAD_DIM, 2, dtype=F32) / HEAD_DIM)
    ang = pos.astype(F32)[:, None] * inv_freq[None, :]
    cos, sin = jnp.cos(ang), jnp.sin(ang)
    tab = jnp.concatenate([cos, cos, -sin, sin], axis=-1)
    if l < m:
        tab = jnp.tile(tab, (m // l, 1))
    return tab


def _trunk(x, pos_offset, s_ret, s_ssm, s_lru, c_conv, w, scan_params):
    b, l, _ = x.shape
    n = b * l
    m = min(ROW_BLOCK, n)
    tb = min(RET_BLOCK, l)
    tt = min(SCAN_BLOCK, l)
    nseq = min(b, max(1, RET_ROWS // tb))
    assert n % m == 0 and (l % m == 0 or m % l == 0) and l % tb == 0 and l % tt == 0
    assert b % SEQ_TILE == 0 and b % nseq == 0
    rot = _rotary_table(pos_offset, l, m)
    nper = max(l // m, 1)
    tables = _ret_tables(tb)
    depth = len(scan_params)
    xs = x
    new_ssm, new_lru, new_conv = [], [], []
    if s_ret is None:
        s_ret = jnp.zeros((1, nseq, RET_HEADS, HEAD_DIM, HEAD_DIM), F32)
    ret_all = jnp.zeros((depth, b, RET_HEADS, HEAD_DIM, HEAD_DIM), F32)
    for i in range(depth):
        qkv, g, sc = _proj_call(xs.reshape(n, D_MODEL), w['norm_mix'], w['w_in'], rot, m, nper, i)
        oret, ret_all = _ret_call(qkv.reshape(b, l, 3 * D_RET), g.reshape(b, l, D_RET), s_ret, ret_all,
                                  tables, tb, nseq, i)
        hs0 = jnp.concatenate([s_ssm[i][..., 0].reshape(b, N_SSM), s_ssm[i][..., 1].reshape(b, N_SSM)], axis=1)
        xs, hs, hl, cc = _mix_call(sc.reshape(b, l, D_SCAN), xs, oret, hs0, s_lru[i],
                                   jnp.swapaxes(c_conv[i], 0, 1), scan_params[i], w, tt, i,
                                   final=(i == depth - 1))
        new_ssm.append(jnp.stack([hs[:, :N_SSM].reshape(b, SSM_GROUPS, SSM_STATE),
                                  hs[:, N_SSM:].reshape(b, SSM_GROUPS, SSM_STATE)], axis=-1))
        new_lru.append(hl)
        new_conv.append(jnp.swapaxes(cc, 0, 1))
    return (xs, ret_all, jnp.stack(new_ssm),
            jnp.stack(new_lru), jnp.stack(new_conv))


def _prepare_weights(norm_mix, w_in, mix_scale, w_out, ssm_w_glu, norm_ffn, w_ffn_gate, w_ffn_up, w_ffn_down,
                     norm_final):
    depth = norm_mix.shape[0]
    rows = lambda t: t.reshape(depth, 1, D_MODEL)
    return {
        'norm_mix': rows(norm_mix), 'w_in': w_in.astype(BF16),
        'mix_scale': rows(mix_scale), 'w_out': w_out.astype(BF16), 'w_glu': ssm_w_glu.astype(BF16),
        'norm_ffn': rows(norm_ffn), 'w_gate': w_ffn_gate.astype(BF16), 'w_up': w_ffn_up.astype(BF16),
        'w_down': w_ffn_down.astype(BF16), 'norm_final': norm_final.reshape(1, D_MODEL),
    }


def kernel(x_prompt, x_sample, state_ret, state_ssm, state_lru, cache_conv, norm_mix, w_in, mix_scale, w_out, ssm_a_re, ssm_a_im, ssm_b_re, ssm_b_im, ssm_c_re, ssm_c_im, ssm_d, ssm_log_dt, ssm_w_glu, lru_conv_w, lru_conv_b, lru_w_a, lru_b_a, lru_w_x, lru_b_x, lru_lambda, norm_ffn, w_ffn_gate, w_ffn_up, w_ffn_down, norm_final):
    w = _prepare_weights(norm_mix, w_in, mix_scale, w_out, ssm_w_glu, norm_ffn, w_ffn_gate, w_ffn_up, w_ffn_down,
                         norm_final)
    depth = norm_mix.shape[0]
    scan_params = [
        _scan_params(ssm_a_re[i], ssm_a_im[i], ssm_b_re[i], ssm_b_im[i], ssm_c_re[i], ssm_c_im[i],
                     ssm_d[i], ssm_log_dt[i], lru_conv_w[i], lru_conv_b[i],
                     lru_w_a[i], lru_b_a[i], lru_w_x[i], lru_b_x[i], lru_lambda[i])
        for i in range(depth)]
    bp = x_prompt.shape[0]
    z_ssm = jnp.zeros((depth, bp, SSM_GROUPS, SSM_STATE, 2), F32)
    z_lru = jnp.zeros((depth, bp, D_LRU), F32)
    z_conv = jnp.zeros((depth, bp, CONV_W - 1, D_LRU), F32)
    out_p = _trunk(x_prompt, 0, None, z_ssm, z_lru, z_conv, w, scan_params)
    out_s = _trunk(x_sample, PAST_LEN, state_ret, state_ssm, state_lru, cache_conv, w, scan_params)
    return (out_p[0], out_s[0]) + tuple(out_p[1:]) + tuple(out_s[1:])
```

```python
import functools
import math

import jax
import jax.numpy as jnp
from jax import lax
from jax.experimental import pallas as pl
from jax.experimental.pallas import tpu as pltpu

F32 = jnp.float32
BF16 = jnp.bfloat16

D_MODEL = 1024
D_RET = 512
D_SSM = 256
D_LRU = 256
RET_HEADS = 4
HEAD_DIM = 128
RET_CHUNK = 64
ROPE_BASE = 10000.0
SSM_GROUP = 16
SSM_GROUPS = 16
SSM_STATE = 64
N_SSM = SSM_GROUPS * SSM_STATE
LRU_BLOCKS = 4
CONV_W = 4
LRU_C = 8.0
D_FFN = 2816
D_IN = 4 * D_RET + D_SSM + 2 * D_LRU
D_SCAN = D_SSM + 2 * D_LRU
EPS = 1e-6
GN_EPS = 1e-5

LANES = 128
SEQ_TILE = 8
ROW_BLOCK = 512
RET_BLOCK = 256
RET_ROWS = 1024
PAST_LEN = 1024
SCAN_BLOCK = 64
FFN_SPLITS = ((0, 1536), (1536, D_FFN))
VMEM_LIMIT = 60 * 1024 * 1024


def _const_spec(shape):
    nd = len(shape)
    return pl.BlockSpec(shape, lambda *_: (0,) * nd, pipeline_mode=pl.Buffered(1))


def _layer_spec(shape, layer):
    nd = len(shape)
    return pl.BlockSpec((None,) + tuple(shape), lambda *_: (layer,) + (0,) * nd, pipeline_mode=pl.Buffered(1))


def _sigmoid(x):
    return 0.5 * (1.0 + jnp.tanh(0.5 * x))


def _gelu_tanh(x):
    return 0.5 * x * (1.0 + jnp.tanh(math.sqrt(2.0 / math.pi) * (x + 0.044715 * (x * x * x))))


def _rms_rows(x, eps):
    return x * lax.rsqrt(jnp.mean(x * x, axis=-1, keepdims=True) + eps)


def _proj_kernel(x_ref, nw_ref, w_ref, rot_ref, qkv_ref, g_ref, sc_ref):
    x = x_ref[...]
    h = (_rms_rows(x, EPS) * nw_ref[...]).astype(BF16)
    proj = jnp.dot(h, w_ref[...], preferred_element_type=F32)
    cos2 = rot_ref[:, :HEAD_DIM]
    sin2 = rot_ref[:, HEAD_DIM:]
    q_scale = HEAD_DIM ** -0.5
    for head in range(2 * RET_HEADS):
        lo = head * HEAD_DIM
        xh = proj[:, lo:lo + HEAD_DIM]
        r = xh * cos2 + pltpu.roll(xh, HEAD_DIM // 2, axis=1) * sin2
        if head < RET_HEADS:
            r = r * q_scale
        qkv_ref[:, lo:lo + HEAD_DIM] = r.astype(BF16)
    qkv_ref[:, 2 * D_RET:3 * D_RET] = proj[:, 2 * D_RET:3 * D_RET].astype(BF16)
    gg = proj[:, 3 * D_RET:4 * D_RET]
    g_ref[...] = gg * _sigmoid(gg)
    sc_ref[:, :D_SSM + D_LRU] = proj[:, 4 * D_RET:4 * D_RET + D_SSM + D_LRU]
    sc_ref[:, D_SSM + D_LRU:] = _gelu_tanh(proj[:, 4 * D_RET + D_SSM + D_LRU:])


def _proj_call(x2d, nw, w_in, rot, m, nper, layer):
    n = x2d.shape[0]
    return pl.pallas_call(
        _proj_kernel,
        grid=(n // m,),
        in_specs=[
            pl.BlockSpec((m, D_MODEL), lambda i: (i, 0)),
            _layer_spec((1, D_MODEL), layer),
            _layer_spec((D_MODEL, D_IN), layer),
            pl.BlockSpec((m, 2 * HEAD_DIM), lambda i: (i % nper, 0)),
        ],
        out_specs=[
            pl.BlockSpec((m, 3 * D_RET), lambda i: (i, 0)),
            pl.BlockSpec((m, D_RET), lambda i: (i, 0)),
            pl.BlockSpec((m, D_SCAN), lambda i: (i, 0)),
        ],
        out_shape=[
            jax.ShapeDtypeStruct((n, 3 * D_RET), BF16),
            jax.ShapeDtypeStruct((n, D_RET), F32),
            jax.ShapeDtypeStruct((n, D_SCAN), F32),
        ],
        compiler_params=pltpu.CompilerParams(
            dimension_semantics=("arbitrary",), vmem_limit_bytes=VMEM_LIMIT),
        name="proj",
    )(x2d, nw, w_in, rot)


def _ret_kernel(qkv_ref, g_ref, s0_ref, s_all_ref, dmask_ref, qdec_ref, kdec_ref, gblk_ref,
                o_ref, sfin_ref, s_scr, *, nseq):
    del s_all_ref
    c = pl.program_id(1)

    @pl.when(c == 0)
    def _():
        s_scr[...] = s0_ref[...]

    for j in range(nseq):
        for h in range(RET_HEADS):
            lo = h * HEAD_DIM
            q = qkv_ref[j, :, lo:lo + HEAD_DIM]
            k = qkv_ref[j, :, D_RET + lo:D_RET + lo + HEAD_DIM]
            v = qkv_ref[j, :, 2 * D_RET + lo:2 * D_RET + lo + HEAD_DIM]
            s = s_scr[j, h]
            scores = lax.dot_general(q, k, (((1,), (1,)), ((), ())), preferred_element_type=F32)
            a = (scores * dmask_ref[h]).astype(BF16)
            o = jnp.dot(a, v, preferred_element_type=F32)
            o = o + jnp.dot(q, s.astype(BF16), preferred_element_type=F32) * qdec_ref[h]
            kd = (k.astype(F32) * kdec_ref[h]).astype(BF16)
            u = lax.dot_general(kd, v, (((0,), (0,)), ((), ())), preferred_element_type=F32)
            s_scr[j, h] = gblk_ref[h] * s + u
            oc = o - jnp.mean(o, axis=-1, keepdims=True)
            on = oc * lax.rsqrt(jnp.mean(oc * oc, axis=-1, keepdims=True) + GN_EPS)
            o_ref[j, :, lo:lo + HEAD_DIM] = on * g_ref[j, :, lo:lo + HEAD_DIM]

    @pl.when(c == pl.num_programs(1) - 1)
    def _():
        sfin_ref[...] = s_scr[...]


def _ret_tables(tb):
    h = jnp.arange(RET_HEADS, dtype=F32)
    gamma = 1.0 - 2.0 ** (-5.0 - h)
    log_g = jnp.log(gamma)
    idx = jnp.arange(tb, dtype=F32)
    chunk = jnp.arange(tb, dtype=jnp.int32) // RET_CHUNK
    diff = idx[:, None] - idx[None, :]
    same = chunk[:, None] == chunk[None, :]
    earlier = chunk[:, None] > chunk[None, :]
    expo = jnp.where(same, jnp.abs(diff), diff)
    dmask = jnp.where((same | earlier)[None], jnp.exp(log_g[:, None, None] * expo[None]), 0.0)
    qdec = jnp.exp(log_g[:, None] * (idx + 1.0)[None, :])
    kdec = jnp.exp(log_g[:, None] * (tb - 1.0 - idx)[None, :])
    gblk = jnp.exp(log_g * tb)
    wide = lambda t: jnp.broadcast_to(t[:, :, None], t.shape + (HEAD_DIM,))
    return dmask, wide(qdec), wide(kdec), jnp.broadcast_to(gblk[:, None, None], (RET_HEADS, 1, HEAD_DIM))


def _ret_call(qkv, g, s0, s_all, tables, tb, nseq, layer):
    b, l, _ = qkv.shape
    dmask, qdec, kdec, gblk = tables
    state_block = (None, nseq, RET_HEADS, HEAD_DIM, HEAD_DIM)
    if s0.shape[0] == 1:
        s0_spec = pl.BlockSpec(state_block, lambda i, c: (0, 0, 0, 0, 0))
    else:
        s0_spec = pl.BlockSpec(state_block, lambda i, c: (layer, i, 0, 0, 0))
    return pl.pallas_call(
        functools.partial(_ret_kernel, nseq=nseq),
        grid=(b // nseq, l // tb),
        in_specs=[
            pl.BlockSpec((nseq, tb, 3 * D_RET), lambda i, c: (i, c, 0)),
            pl.BlockSpec((nseq, tb, D_RET), lambda i, c: (i, c, 0)),
            s0_spec,
            pl.BlockSpec(memory_space=pl.ANY),
            _const_spec((RET_HEADS, tb, tb)),
            _const_spec((RET_HEADS, tb, HEAD_DIM)),
            _const_spec((RET_HEADS, tb, HEAD_DIM)),
            _const_spec((RET_HEADS, 1, HEAD_DIM)),
        ],
        out_specs=[
            pl.BlockSpec((nseq, tb, D_RET), lambda i, c: (i, c, 0)),
            pl.BlockSpec(state_block, lambda i, c: (layer, i, 0, 0, 0)),
        ],
        out_shape=[
            jax.ShapeDtypeStruct((b, l, D_RET), F32),
            jax.ShapeDtypeStruct(s_all.shape, F32),
        ],
        input_output_aliases={3: 1},
        scratch_shapes=[pltpu.VMEM((nseq, RET_HEADS, HEAD_DIM, HEAD_DIM), F32)],
        compiler_params=pltpu.CompilerParams(
            dimension_semantics=("arbitrary", "arbitrary"), vmem_limit_bytes=VMEM_LIMIT),
        name="ret",
    )(qkv, g, s0, s_all, dmask, qdec, kdec, gblk)


def _scan_pre(sc_ref, params, scratch, tt):
    bmat_ref, lam_ref, cmat_ref, dskip_ref, convw_ref, convb_ref, wa_ref, ba_ref, wx_ref, bxb_ref, sp_ref = params
    tm, hbuf, hst, xpad, abuf, bbuf, lst, res = scratch
    rows = tt * SEQ_TILE
    pad_rows = (CONV_W - 1) * SEQ_TILE

    for s in range(SEQ_TILE):
        for j in range(D_SCAN // LANES):
            tm[j, pl.ds(s, tt, stride=SEQ_TILE), :] = sc_ref[s, :, j * LANES:(j + 1) * LANES]

    u = jnp.concatenate([tm[0], tm[1]], axis=-1)
    hbuf[...] = jnp.dot(u.astype(BF16), bmat_ref[...], preferred_element_type=F32)

    xpad[pad_rows:, :] = jnp.concatenate([tm[2], tm[3]], axis=-1)
    xc = convb_ref[...] + xpad[0:rows, :] * convw_ref[0:1, :]
    for j in range(1, CONV_W):
        xc = xc + xpad[j * SEQ_TILE:j * SEQ_TILE + rows, :] * convw_ref[j:j + 1, :]
    xcb = xc.astype(BF16)
    r = _sigmoid(jnp.dot(xcb, wa_ref[...], preferred_element_type=F32) + ba_ref[...])
    ig = _sigmoid(jnp.dot(xcb, wx_ref[...], preferred_element_type=F32) + bxb_ref[...])
    log_a = (-LRU_C) * r * sp_ref[...]
    a = jnp.exp(log_a)
    abuf[...] = a
    t1 = 1.0 - a * a
    bbuf[...] = jnp.where(t1 > 0.0, t1 * lax.rsqrt(t1), 0.0) * (ig * xc)

    return hst[:, 0:N_SSM], hst[:, N_SSM:2 * N_SSM], lst[...]


def _scan_steps(params, scratch, carry, t0, t1):
    lam_ref = params[1]
    tm, hbuf, hst, xpad, abuf, bbuf, lst, res = scratch
    lam_re = lam_ref[0:1, :]
    lam_im = lam_ref[1:2, :]
    hr, hi, hl = carry
    for t in range(t0, t1):
        lo = t * SEQ_TILE
        br = hbuf[lo:lo + SEQ_TILE, 0:N_SSM]
        bi = hbuf[lo:lo + SEQ_TILE, N_SSM:2 * N_SSM]
        hr, hi = lam_re * hr - lam_im * hi + br, lam_re * hi + lam_im * hr + bi
        hbuf[lo:lo + SEQ_TILE, 0:N_SSM] = hr
        hbuf[lo:lo + SEQ_TILE, N_SSM:2 * N_SSM] = hi
        hl = abuf[lo:lo + SEQ_TILE, :] * hl + bbuf[lo:lo + SEQ_TILE, :]
        bbuf[lo:lo + SEQ_TILE, :] = hl
    return hr, hi, hl


def _scan_post(params, scratch, carry, osl_out, tt):
    bmat_ref, lam_ref, cmat_ref, dskip_ref = params[:4]
    tm, hbuf, hst, xpad, abuf, bbuf, lst, res = scratch
    rows = tt * SEQ_TILE
    hr, hi, hl = carry
    hst[:, 0:N_SSM] = hr
    hst[:, N_SSM:2 * N_SSM] = hi
    lst[...] = hl
    for j in range(CONV_W - 1):
        xpad[j * SEQ_TILE:(j + 1) * SEQ_TILE, :] = xpad[rows + j * SEQ_TILE:rows + (j + 1) * SEQ_TILE, :]

    u = jnp.concatenate([tm[0], tm[1]], axis=-1)
    y = jnp.dot(hbuf[...].astype(BF16), cmat_ref[...], preferred_element_type=F32) + dskip_ref[...] * u
    o_lru = bbuf[...] * jnp.concatenate([tm[4], tm[5]], axis=-1)
    res[0] = y[:, :LANES]
    res[1] = y[:, LANES:]
    res[2] = o_lru[:, :LANES]
    res[3] = o_lru[:, LANES:]
    for s in range(SEQ_TILE):
        for j in range((D_SSM + D_LRU) // LANES):
            osl_out[s * tt:(s + 1) * tt, j * LANES:(j + 1) * LANES] = res[j, pl.ds(s, tt, stride=SEQ_TILE), :]


def _merge_head(x, oret, osl, weights):
    wglu_ref, ms_ref, wo_ref, nf_ref = weights[:4]
    ga = jnp.dot(_gelu_tanh(osl[:, :D_SSM]).astype(BF16), wglu_ref[...], preferred_element_type=F32)
    o_ssm = ga[:, :D_SSM] * _sigmoid(ga[:, D_SSM:])
    mixed = jnp.concatenate(
        [_rms_rows(oret, EPS), _rms_rows(o_ssm, EPS), _rms_rows(osl[:, D_SSM:], EPS)], axis=-1) * ms_ref[...]
    x = x + jnp.dot(mixed.astype(BF16), wo_ref[...], preferred_element_type=F32)
    return x, (_rms_rows(x, EPS) * nf_ref[...]).astype(BF16)


def _ffn_group(x, h, weights, lo, hi):
    wg_ref, wu_ref, wd_ref = weights[4:7]
    gate = jnp.dot(h, wg_ref[:, lo:hi], preferred_element_type=F32)
    up = jnp.dot(h, wu_ref[:, lo:hi], preferred_element_type=F32)
    ff = (gate * _sigmoid(gate) * up).astype(BF16)
    return x + jnp.dot(ff, wd_ref[lo:hi, :], preferred_element_type=F32)


def _mix_kernel(*refs, tt, nt, nb, final):
    sc_ref, x_ref, oret_ref, hs0_ref, hl0_ref, cc0_ref = refs[:6]
    scan_params = refs[6:17]
    merge_weights = refs[17:25]
    y_ref, hs_ref, hl_ref, cc_ref = refs[25:29]
    scratch = refs[29:37]
    osl_cur, osl_prev = refs[37:39]
    tm, hbuf, hst, xpad, abuf, bbuf, lst, res = scratch
    k = pl.program_id(0)
    c = lax.rem(k, nt)
    rows = tt * SEQ_TILE

    @pl.when(k == 0)
    def _():
        osl_prev[...] = jnp.zeros(osl_prev.shape, F32)

    @pl.when((c == 0) & (k < nb))
    def _():
        hst[...] = hs0_ref[...]
        lst[...] = hl0_ref[...]
        for j in range(CONV_W - 1):
            xpad[j * SEQ_TILE:(j + 1) * SEQ_TILE, :] = cc0_ref[j]

    carry = _scan_pre(sc_ref, scan_params, scratch, tt)
    carry = _scan_steps(scan_params, scratch, carry, 0, tt)
    _scan_post(scan_params, scratch, carry, osl_cur, tt)

    x, h = _merge_head(x_ref[...].reshape(rows, D_MODEL), oret_ref[...].reshape(rows, D_RET),
                       osl_prev[...], merge_weights)
    for lo, hi in FFN_SPLITS:
        x = _ffn_group(x, h, merge_weights, lo, hi)
    if final:
        x = _rms_rows(x, EPS) * merge_weights[7][...]
    y_ref[...] = x.reshape(SEQ_TILE, tt, D_MODEL)

    osl_prev[...] = osl_cur[...]

    @pl.when((c == nt - 1) & (k < nb))
    def _():
        hs_ref[...] = hst[...]
        hl_ref[...] = lst[...]
        for j in range(CONV_W - 1):
            cc_ref[j] = xpad[j * SEQ_TILE:(j + 1) * SEQ_TILE, :]


def _mix_call(sc, x, oret, hs0, hl0, cc0, sp, w, tt, layer, final):
    b, l, _ = sc.shape
    nt = l // tt
    nb = (b // SEQ_TILE) * nt
    rows = tt * SEQ_TILE
    scanned = lambda k: jnp.minimum(k, nb - 1)
    merged = lambda k: jnp.maximum(k - 1, 0)
    blk = lambda f, width: pl.BlockSpec((SEQ_TILE, tt, width), lambda k: (f(k) // nt, f(k) % nt, 0))
    seq_spec = lambda width: pl.BlockSpec((SEQ_TILE, width), lambda k: (scanned(k) // nt, 0))
    cc_spec = pl.BlockSpec((CONV_W - 1, SEQ_TILE, D_LRU), lambda k: (0, scanned(k) // nt, 0))
    return pl.pallas_call(
        functools.partial(_mix_kernel, tt=tt, nt=nt, nb=nb, final=final),
        grid=(nb + 1,),
        in_specs=[
            blk(scanned, D_SCAN), blk(merged, D_MODEL), blk(merged, D_RET),
            seq_spec(2 * N_SSM), seq_spec(D_LRU), cc_spec,
            _const_spec((D_SSM, 2 * N_SSM)), _const_spec((2, N_SSM)), _const_spec((2 * N_SSM, D_SSM)),
            _const_spec((1, D_SSM)),
            _const_spec((CONV_W, D_LRU)), _const_spec((1, D_LRU)),
            _const_spec((D_LRU, D_LRU)), _const_spec((1, D_LRU)),
            _const_spec((D_LRU, D_LRU)), _const_spec((1, D_LRU)), _const_spec((1, D_LRU)),
            _layer_spec((D_SSM, 2 * D_SSM), layer),
            _layer_spec((1, D_MODEL), layer), _layer_spec((D_MODEL, D_MODEL), layer),
            _layer_spec((1, D_MODEL), layer),
            _layer_spec((D_MODEL, D_FFN), layer), _layer_spec((D_MODEL, D_FFN), layer),
            _layer_spec((D_FFN, D_MODEL), layer),
            _const_spec((1, D_MODEL)),
        ],
        out_specs=[
            blk(merged, D_MODEL),
            seq_spec(2 * N_SSM), seq_spec(D_LRU), cc_spec,
        ],
        out_shape=[
            jax.ShapeDtypeStruct((b, l, D_MODEL), F32),
            jax.ShapeDtypeStruct((b, 2 * N_SSM), F32),
            jax.ShapeDtypeStruct((b, D_LRU), F32),
            jax.ShapeDtypeStruct((CONV_W - 1, b, D_LRU), F32),
        ],
        scratch_shapes=[
            pltpu.VMEM((D_SCAN // LANES, rows, LANES), F32),
            pltpu.VMEM((rows, 2 * N_SSM), F32),
            pltpu.VMEM((SEQ_TILE, 2 * N_SSM), F32),
            pltpu.VMEM((rows + (CONV_W - 1) * SEQ_TILE, D_LRU), F32),
            pltpu.VMEM((rows, D_LRU), F32),
            pltpu.VMEM((rows, D_LRU), F32),
            pltpu.VMEM((SEQ_TILE, D_LRU), F32),
            pltpu.VMEM(((D_SSM + D_LRU) // LANES, rows, LANES), F32),
            pltpu.VMEM((rows, D_SSM + D_LRU), F32),
            pltpu.VMEM((rows, D_SSM + D_LRU), F32),
        ],
        compiler_params=pltpu.CompilerParams(
            dimension_semantics=("arbitrary",), vmem_limit_bytes=VMEM_LIMIT),
        name="mix",
    )(sc, x, oret, hs0, hl0, cc0, *sp,
      w['w_glu'], w['mix_scale'], w['w_out'], w['norm_ffn'], w['w_gate'], w['w_up'], w['w_down'], w['norm_final'])


def _scan_params(a_re, a_im, b_re, b_im, c_re, c_im, d_skip, log_dt,
                 conv_w, conv_b, w_a, b_a, w_x, b_x, lam):
    dt = jnp.exp(log_dt)[:, None]
    mag = jnp.exp(a_re * dt)
    lb_re = mag * jnp.cos(a_im * dt)
    lb_im = mag * jnp.sin(a_im * dt)
    den = a_re * a_re + a_im * a_im
    f_re = ((lb_re - 1.0) * a_re + lb_im * a_im) / den
    f_im = (lb_im * a_re - (lb_re - 1.0) * a_im) / den
    bb_re = f_re[..., None] * b_re - f_im[..., None] * b_im
    bb_im = f_re[..., None] * b_im + f_im[..., None] * b_re
    eye_g = jnp.eye(SSM_GROUPS, dtype=F32)
    bd_in = lambda t: jnp.einsum('gpc,gh->gchp', t, eye_g).reshape(D_SSM, N_SSM)
    bmat = jnp.concatenate([bd_in(bb_re), bd_in(bb_im)], axis=1).astype(BF16)
    bd_out = lambda t: jnp.einsum('gcp,gh->gphc', t, eye_g).reshape(N_SSM, D_SSM)
    cmat = jnp.concatenate([bd_out(c_re), bd_out(-c_im)], axis=0).astype(BF16)
    lam_bar = jnp.stack([lb_re.reshape(N_SSM), lb_im.reshape(N_SSM)])
    eye_b = jnp.eye(LRU_BLOCKS, dtype=F32)
    bd_lru = lambda t: jnp.einsum('hij,hk->hikj', t, eye_b).reshape(D_LRU, D_LRU).astype(BF16)
    row = lambda t: t.reshape(1, -1)
    return (bmat, lam_bar, cmat, row(d_skip),
            conv_w, row(conv_b), bd_lru(w_a), row(b_a), bd_lru(w_x), row(b_x),
            row(jax.nn.softplus(-lam)))


def _rotary_table(pos_offset, l, m):
    pos = pos_offset + jnp.arange(l, dtype=jnp.int32)
    inv_freq = ROPE_BASE ** (-jnp.arange(0, HEAD_DIM, 2, dtype=F32) / HEAD_DIM)
    ang = pos.astype(F32)[:, None] * inv_freq[None, :]
    cos, sin = jnp.cos(ang), jnp.sin(ang)
    tab = jnp.concatenate([cos, cos, -sin, sin], axis=-1)
    if l < m:
        tab = jnp.tile(tab, (m // l, 1))
    return tab


def _trunk(x, pos_offset, s_ret, s_ssm, s_lru, c_conv, w, scan_params):
    b, l, _ = x.shape
    n = b * l
    m = min(ROW_BLOCK, n)
    tb = min(RET_BLOCK, l)
    tt = min(SCAN_BLOCK, l)
    nseq = min(b, max(1, RET_ROWS // tb))
    assert n % m == 0 and (l % m == 0 or m % l == 0) and l % tb == 0 and l % tt == 0
    assert b % SEQ_TILE == 0 and b % nseq == 0
    rot = _rotary_table(pos_offset, l, m)
    nper = max(l // m, 1)
    tables = _ret_tables(tb)
    depth = len(scan_params)
    xs = x
    new_ssm, new_lru, new_conv = [], [], []
    if s_ret is None:
        s_ret = jnp.zeros((1, nseq, RET_HEADS, HEAD_DIM, HEAD_DIM), F32)
    ret_all = jnp.zeros((depth, b, RET_HEADS, HEAD_DIM, HEAD_DIM), F32)
    for i in range(depth):
        qkv, g, sc = _proj_call(xs.reshape(n, D_MODEL), w['norm_mix'], w['w_in'], rot, m, nper, i)
        oret, ret_all = _ret_call(qkv.reshape(b, l, 3 * D_RET), g.reshape(b, l, D_RET), s_ret, ret_all,
                                  tables, tb, nseq, i)
        hs0 = jnp.concatenate([s_ssm[i][..., 0].reshape(b, N_SSM), s_ssm[i][..., 1].reshape(b, N_SSM)], axis=1)
        xs, hs, hl, cc = _mix_call(sc.reshape(b, l, D_SCAN), xs, oret, hs0, s_lru[i],
                                   jnp.swapaxes(c_conv[i], 0, 1), scan_params[i], w, tt, i,
                                   final=(i == depth - 1))
        new_ssm.append(jnp.stack([hs[:, :N_SSM].reshape(b, SSM_GROUPS, SSM_STATE),
                                  hs[:, N_SSM:].reshape(b, SSM_GROUPS, SSM_STATE)], axis=-1))
        new_lru.append(hl)
        new_conv.append(jnp.swapaxes(cc, 0, 1))
    return (xs, ret_all, jnp.stack(new_ssm),
            jnp.stack(new_lru), jnp.stack(new_conv))


def _prepare_weights(norm_mix, w_in, mix_scale, w_out, ssm_w_glu, norm_ffn, w_ffn_gate, w_ffn_up, w_ffn_down,
                     norm_final):
    depth = norm_mix.shape[0]
    rows = lambda t: t.reshape(depth, 1, D_MODEL)
    return {
        'norm_mix': rows(norm_mix), 'w_in': w_in.astype(BF16),
        'mix_scale': rows(mix_scale), 'w_out': w_out.astype(BF16), 'w_glu': ssm_w_glu.astype(BF16),
        'norm_ffn': rows(norm_ffn), 'w_gate': w_ffn_gate.astype(BF16), 'w_up': w_ffn_up.astype(BF16),
        'w_down': w_ffn_down.astype(BF16), 'norm_final': norm_final.reshape(1, D_MODEL),
    }


def kernel(x_prompt, x_sample, state_ret, state_ssm, state_lru, cache_conv, norm_mix, w_in, mix_scale, w_out, ssm_a_re, ssm_a_im, ssm_b_re, ssm_b_im, ssm_c_re, ssm_c_im, ssm_d, ssm_log_dt, ssm_w_glu, lru_conv_w, lru_conv_b, lru_w_a, lru_b_a, lru_w_x, lru_b_x, lru_lambda, norm_ffn, w_ffn_gate, w_ffn_up, w_ffn_down, norm_final):
    w = _prepare_weights(norm_mix, w_in, mix_scale, w_out, ssm_w_glu, norm_ffn, w_ffn_gate, w_ffn_up, w_ffn_down,
                         norm_final)
    depth = norm_mix.shape[0]
    scan_params = [
        _scan_params(ssm_a_re[i], ssm_a_im[i], ssm_b_re[i], ssm_b_im[i], ssm_c_re[i], ssm_c_im[i],
                     ssm_d[i], ssm_log_dt[i], lru_conv_w[i], lru_conv_b[i],
                     lru_w_a[i], lru_b_a[i], lru_w_x[i], lru_b_x[i], lru_lambda[i])
        for i in range(depth)]
    bp = x_prompt.shape[0]
    z_ssm = jnp.zeros((depth, bp, SSM_GROUPS, SSM_STATE, 2), F32)
    z_lru = jnp.zeros((depth, bp, D_LRU), F32)
    z_conv = jnp.zeros((depth, bp, CONV_W - 1, D_LRU), F32)
    out_p = _trunk(x_prompt, 0, None, z_ssm, z_lru, z_conv, w, scan_params)
    out_s = _trunk(x_sample, PAST_LEN, state_ret, state_ssm, state_lru, cache_conv, w, scan_params)
    return (out_p[0], out_s[0]) + tuple(out_p[1:]) + tuple(out_s[1:])
```

```python
import functools
import math

import jax
import jax.numpy as jnp
from jax import lax
from jax.experimental import pallas as pl
from jax.experimental.pallas import tpu as pltpu

F32 = jnp.float32
BF16 = jnp.bfloat16

D_MODEL = 1024
D_RET = 512
D_SSM = 256
D_LRU = 256
RET_HEADS = 4
HEAD_DIM = 128
RET_CHUNK = 64
ROPE_BASE = 10000.0
SSM_GROUP = 16
SSM_GROUPS = 16
SSM_STATE = 64
N_SSM = SSM_GROUPS * SSM_STATE
LRU_BLOCKS = 4
CONV_W = 4
LRU_C = 8.0
D_FFN = 2816
D_IN = 4 * D_RET + D_SSM + 2 * D_LRU
D_SCAN = D_SSM + 2 * D_LRU
EPS = 1e-6
GN_EPS = 1e-5

LANES = 128
SEQ_TILE = 8
PROJ_SEQS = 2
RET_BLOCK = 256
RET_ROWS = 1024
RET_SEQS_MAX = 8
PAST_LEN = 1024
SCAN_BLOCK = 64
FFN_SPLITS = ((0, 1536), (1536, D_FFN))
VMEM_LIMIT = 60 * 1024 * 1024


def _const_spec(shape):
    nd = len(shape)
    return pl.BlockSpec(shape, lambda *_: (0,) * nd, pipeline_mode=pl.Buffered(1))


def _layer_spec(shape, layer):
    nd = len(shape)
    return pl.BlockSpec((None,) + tuple(shape), lambda *_: (layer,) + (0,) * nd, pipeline_mode=pl.Buffered(1))


def _sigmoid(x):
    return 0.5 * (1.0 + jnp.tanh(0.5 * x))


def _gelu_tanh(x):
    return 0.5 * x * (1.0 + jnp.tanh(math.sqrt(2.0 / math.pi) * (x + 0.044715 * (x * x * x))))


def _rms_rows(x, eps):
    return x * lax.rsqrt(jnp.mean(x * x, axis=-1, keepdims=True) + eps)


def _proj_block(x, nw_ref, w_ref, rot_ref, qkv_out, g_out, sc_ref, nseq, tb):
    cos2 = rot_ref[:, :HEAD_DIM]
    sin2 = rot_ref[:, HEAD_DIM:]
    q_scale = HEAD_DIM ** -0.5
    h = (_rms_rows(x, EPS) * nw_ref[...]).astype(BF16)
    for j0 in range(0, nseq, PROJ_SEQS):
        j1 = min(j0 + PROJ_SEQS, nseq)
        proj = jnp.dot(h[j0 * tb:j1 * tb], w_ref[...], preferred_element_type=F32)
        for j in range(j0, j1):
            pj = proj[(j - j0) * tb:(j - j0 + 1) * tb]
            for head in range(2 * RET_HEADS):
                lo = head * HEAD_DIM
                xh = pj[:, lo:lo + HEAD_DIM]
                r = xh * cos2 + pltpu.roll(xh, HEAD_DIM // 2, axis=1) * sin2
                if head < RET_HEADS:
                    r = r * q_scale
                qkv_out[j, :, lo:lo + HEAD_DIM] = r.astype(BF16)
            qkv_out[j, :, 2 * D_RET:3 * D_RET] = pj[:, 2 * D_RET:3 * D_RET].astype(BF16)
            gg = pj[:, 3 * D_RET:4 * D_RET]
            g_out[j] = gg * _sigmoid(gg)
            sc_ref[j, :, :D_SSM + D_LRU] = pj[:, 4 * D_RET:4 * D_RET + D_SSM + D_LRU]
            sc_ref[j, :, D_SSM + D_LRU:] = _gelu_tanh(pj[:, 4 * D_RET + D_SSM + D_LRU:])


def _ret_block(qkv, g, tables, s_scr, o_ref, nseq):
    dmask_ref, qdec_ref, kdec_ref, gblk_ref = tables
    for j in range(nseq):
        for h in range(RET_HEADS):
            lo = h * HEAD_DIM
            q = qkv[j, :, lo:lo + HEAD_DIM]
            k = qkv[j, :, D_RET + lo:D_RET + lo + HEAD_DIM]
            v = qkv[j, :, 2 * D_RET + lo:2 * D_RET + lo + HEAD_DIM]
            s = s_scr[j, h]
            scores = lax.dot_general(q, k, (((1,), (1,)), ((), ())), preferred_element_type=F32)
            a = (scores * dmask_ref[h]).astype(BF16)
            o = jnp.dot(a, v, preferred_element_type=F32)
            o = o + jnp.dot(q, s.astype(BF16), preferred_element_type=F32) * qdec_ref[h]
            kd = (k.astype(F32) * kdec_ref[h]).astype(BF16)
            u = lax.dot_general(kd, v, (((0,), (0,)), ((), ())), preferred_element_type=F32)
            s_scr[j, h] = gblk_ref[h] * s + u
            oc = o - jnp.mean(o, axis=-1, keepdims=True)
            on = oc * lax.rsqrt(jnp.mean(oc * oc, axis=-1, keepdims=True) + GN_EPS)
            o_ref[j, :, lo:lo + HEAD_DIM] = on * g[j, :, lo:lo + HEAD_DIM]


def _pr_kernel(x_ref, nw_ref, w_ref, rot_ref, s0_ref, s_all_ref, dmask_ref, qdec_ref, kdec_ref, gblk_ref,
               sc_ref, o_ref, sfin_ref,
               s_scr, qkv_cur, qkv_prev, g_cur, g_prev, *, nseq, tb, nt, nb):
    del s_all_ref
    k = pl.program_id(0)
    c_ret = lax.rem(jnp.maximum(k - 1, 0), nt)

    @pl.when(k == 0)
    def _():
        qkv_prev[...] = jnp.zeros(qkv_prev.shape, BF16)
        g_prev[...] = jnp.zeros(g_prev.shape, F32)

    @pl.when(c_ret == 0)
    def _():
        s_scr[...] = s0_ref[...]

    _proj_block(x_ref[...].reshape(nseq * tb, D_MODEL), nw_ref, w_ref, rot_ref, qkv_cur, g_cur, sc_ref, nseq, tb)
    _ret_block(qkv_prev, g_prev, (dmask_ref, qdec_ref, kdec_ref, gblk_ref), s_scr, o_ref, nseq)
    qkv_prev[...] = qkv_cur[...]
    g_prev[...] = g_cur[...]

    @pl.when((c_ret == nt - 1) & (k >= 1))
    def _():
        sfin_ref[...] = s_scr[...]


def _ret_tables(tb):
    h = jnp.arange(RET_HEADS, dtype=F32)
    gamma = 1.0 - 2.0 ** (-5.0 - h)
    log_g = jnp.log(gamma)
    idx = jnp.arange(tb, dtype=F32)
    chunk = jnp.arange(tb, dtype=jnp.int32) // RET_CHUNK
    diff = idx[:, None] - idx[None, :]
    same = chunk[:, None] == chunk[None, :]
    earlier = chunk[:, None] > chunk[None, :]
    expo = jnp.where(same, jnp.abs(diff), diff)
    dmask = jnp.where((same | earlier)[None], jnp.exp(log_g[:, None, None] * expo[None]), 0.0)
    qdec = jnp.exp(log_g[:, None] * (idx + 1.0)[None, :])
    kdec = jnp.exp(log_g[:, None] * (tb - 1.0 - idx)[None, :])
    gblk = jnp.exp(log_g * tb)
    wide = lambda t: jnp.broadcast_to(t[:, :, None], t.shape + (HEAD_DIM,))
    return dmask, wide(qdec), wide(kdec), jnp.broadcast_to(gblk[:, None, None], (RET_HEADS, 1, HEAD_DIM))


def _pr_call(x, nw, w_in, rot, s0, s0_shared, s_all, tables, tb, nseq, layer):
    b, l, _ = x.shape
    nt = l // tb
    nb = (b // nseq) * nt
    dmask, qdec, kdec, gblk = tables
    projected = lambda k: jnp.minimum(k, nb - 1)
    retained = lambda k: jnp.maximum(k - 1, 0)
    blk = lambda f, width: pl.BlockSpec((nseq, tb, width), lambda k: (f(k) // nt, f(k) % nt, 0))
    state_block = (None, nseq, RET_HEADS, HEAD_DIM, HEAD_DIM)
    if s0_shared:
        s0_spec = pl.BlockSpec(state_block, lambda k: (0, 0, 0, 0, 0))
    else:
        s0_spec = pl.BlockSpec(state_block, lambda k: (layer, retained(k) // nt, 0, 0, 0))
    return pl.pallas_call(
        functools.partial(_pr_kernel, nseq=nseq, tb=tb, nt=nt, nb=nb),
        grid=(nb + 1,),
        in_specs=[
            blk(projected, D_MODEL),
            _layer_spec((1, D_MODEL), layer),
            _layer_spec((D_MODEL, D_IN), layer),
            pl.BlockSpec((tb, 2 * HEAD_DIM), lambda k: (projected(k) % nt, 0)),
            s0_spec,
            pl.BlockSpec(memory_space=pl.ANY),
            _const_spec((RET_HEADS, tb, tb)),
            _const_spec((RET_HEADS, tb, HEAD_DIM)),
            _const_spec((RET_HEADS, tb, HEAD_DIM)),
            _const_spec((RET_HEADS, 1, HEAD_DIM)),
        ],
        out_specs=[
            blk(projected, D_SCAN),
            blk(retained, D_RET),
            pl.BlockSpec(state_block, lambda k: (layer, retained(k) // nt, 0, 0, 0)),
        ],
        out_shape=[
            jax.ShapeDtypeStruct((b, l, D_SCAN), F32),
            jax.ShapeDtypeStruct((b, l, D_RET), F32),
            jax.ShapeDtypeStruct(s_all.shape, F32),
        ],
        input_output_aliases={5: 2},
        scratch_shapes=[
            pltpu.VMEM((nseq, RET_HEADS, HEAD_DIM, HEAD_DIM), F32),
            pltpu.VMEM((nseq, tb, 3 * D_RET), BF16),
            pltpu.VMEM((nseq, tb, 3 * D_RET), BF16),
            pltpu.VMEM((nseq, tb, D_RET), F32),
            pltpu.VMEM((nseq, tb, D_RET), F32),
        ],
        compiler_params=pltpu.CompilerParams(
            dimension_semantics=("arbitrary",), vmem_limit_bytes=VMEM_LIMIT),
        name="pr",
    )(x, nw, w_in, rot, s0, s_all, dmask, qdec, kdec, gblk)


def _scan_pre(sc_ref, params, scratch, tt):
    bmat_ref, lam_ref, cmat_ref, dskip_ref, convw_ref, convb_ref, wa_ref, ba_ref, wx_ref, bxb_ref, sp_ref = params
    tm, hbuf, hst, xpad, abuf, bbuf, lst, res = scratch
    rows = tt * SEQ_TILE
    pad_rows = (CONV_W - 1) * SEQ_TILE

    for s in range(SEQ_TILE):
        for j in range(D_SCAN // LANES):
            tm[j, pl.ds(s, tt, stride=SEQ_TILE), :] = sc_ref[s, :, j * LANES:(j + 1) * LANES]

    u = jnp.concatenate([tm[0], tm[1]], axis=-1)
    hbuf[...] = jnp.dot(u.astype(BF16), bmat_ref[...], preferred_element_type=F32)

    xpad[pad_rows:, :] = jnp.concatenate([tm[2], tm[3]], axis=-1)
    xc = convb_ref[...] + xpad[0:rows, :] * convw_ref[0:1, :]
    for j in range(1, CONV_W):
        xc = xc + xpad[j * SEQ_TILE:j * SEQ_TILE + rows, :] * convw_ref[j:j + 1, :]
    xcb = xc.astype(BF16)
    r = _sigmoid(jnp.dot(xcb, wa_ref[...], preferred_element_type=F32) + ba_ref[...])
    ig = _sigmoid(jnp.dot(xcb, wx_ref[...], preferred_element_type=F32) + bxb_ref[...])
    log_a = (-LRU_C) * r * sp_ref[...]
    a = jnp.exp(log_a)
    abuf[...] = a
    t1 = 1.0 - a * a
    bbuf[...] = jnp.where(t1 > 0.0, t1 * lax.rsqrt(t1), 0.0) * (ig * xc)

    return hst[:, 0:N_SSM], hst[:, N_SSM:2 * N_SSM], lst[...]


def _scan_steps(params, scratch, carry, t0, t1):
    lam_ref = params[1]
    tm, hbuf, hst, xpad, abuf, bbuf, lst, res = scratch
    lam_re = lam_ref[0:1, :]
    lam_im = lam_ref[1:2, :]
    hr, hi, hl = carry
    for t in range(t0, t1):
        lo = t * SEQ_TILE
        br = hbuf[lo:lo + SEQ_TILE, 0:N_SSM]
        bi = hbuf[lo:lo + SEQ_TILE, N_SSM:2 * N_SSM]
        hr, hi = lam_re * hr - lam_im * hi + br, lam_re * hi + lam_im * hr + bi
        hbuf[lo:lo + SEQ_TILE, 0:N_SSM] = hr
        hbuf[lo:lo + SEQ_TILE, N_SSM:2 * N_SSM] = hi
        hl = abuf[lo:lo + SEQ_TILE, :] * hl + bbuf[lo:lo + SEQ_TILE, :]
        bbuf[lo:lo + SEQ_TILE, :] = hl
    return hr, hi, hl


def _scan_post(params, scratch, carry, osl_out, tt):
    bmat_ref, lam_ref, cmat_ref, dskip_ref = params[:4]
    tm, hbuf, hst, xpad, abuf, bbuf, lst, res = scratch
    rows = tt * SEQ_TILE
    hr, hi, hl = carry
    hst[:, 0:N_SSM] = hr
    hst[:, N_SSM:2 * N_SSM] = hi
    lst[...] = hl
    for j in range(CONV_W - 1):
        xpad[j * SEQ_TILE:(j + 1) * SEQ_TILE, :] = xpad[rows + j * SEQ_TILE:rows + (j + 1) * SEQ_TILE, :]

    u = jnp.concatenate([tm[0], tm[1]], axis=-1)
    y = jnp.dot(hbuf[...].astype(BF16), cmat_ref[...], preferred_element_type=F32) + dskip_ref[...] * u
    o_lru = bbuf[...] * jnp.concatenate([tm[4], tm[5]], axis=-1)
    res[0] = y[:, :LANES]
    res[1] = y[:, LANES:]
    res[2] = o_lru[:, :LANES]
    res[3] = o_lru[:, LANES:]
    for s in range(SEQ_TILE):
        for j in range((D_SSM + D_LRU) // LANES):
            osl_out[s * tt:(s + 1) * tt, j * LANES:(j + 1) * LANES] = res[j, pl.ds(s, tt, stride=SEQ_TILE), :]


def _merge_head(x, oret, osl, weights):
    wglu_ref, ms_ref, wo_ref, nf_ref = weights[:4]
    ga = jnp.dot(_gelu_tanh(osl[:, :D_SSM]).astype(BF16), wglu_ref[...], preferred_element_type=F32)
    o_ssm = ga[:, :D_SSM] * _sigmoid(ga[:, D_SSM:])
    mixed = jnp.concatenate(
        [_rms_rows(oret, EPS), _rms_rows(o_ssm, EPS), _rms_rows(osl[:, D_SSM:], EPS)], axis=-1) * ms_ref[...]
    x = x + jnp.dot(mixed.astype(BF16), wo_ref[...], preferred_element_type=F32)
    return x, (_rms_rows(x, EPS) * nf_ref[...]).astype(BF16)


def _ffn_group(x, h, weights, lo, hi):
    wg_ref, wu_ref, wd_ref = weights[4:7]
    gate = jnp.dot(h, wg_ref[:, lo:hi], preferred_element_type=F32)
    up = jnp.dot(h, wu_ref[:, lo:hi], preferred_element_type=F32)
    ff = (gate * _sigmoid(gate) * up).astype(BF16)
    return x + jnp.dot(ff, wd_ref[lo:hi, :], preferred_element_type=F32)


def _mix_kernel(*refs, tt, nt, nb, final):
    sc_ref, x_ref, oret_ref, hs0_ref, hl0_ref, cc0_ref = refs[:6]
    scan_params = refs[6:17]
    merge_weights = refs[17:25]
    y_ref, hs_ref, hl_ref, cc_ref = refs[25:29]
    scratch = refs[29:37]
    osl_cur, osl_prev = refs[37:39]
    tm, hbuf, hst, xpad, abuf, bbuf, lst, res = scratch
    k = pl.program_id(0)
    c = lax.rem(k, nt)
    rows = tt * SEQ_TILE

    @pl.when(k == 0)
    def _():
        osl_prev[...] = jnp.zeros(osl_prev.shape, F32)

    @pl.when((c == 0) & (k < nb))
    def _():
        hst[...] = hs0_ref[...]
        lst[...] = hl0_ref[...]
        for j in range(CONV_W - 1):
            xpad[j * SEQ_TILE:(j + 1) * SEQ_TILE, :] = cc0_ref[j]

    carry = _scan_pre(sc_ref, scan_params, scratch, tt)
    carry = _scan_steps(scan_params, scratch, carry, 0, tt)
    _scan_post(scan_params, scratch, carry, osl_cur, tt)

    x, h = _merge_head(x_ref[...].reshape(rows, D_MODEL), oret_ref[...].reshape(rows, D_RET),
                       osl_prev[...], merge_weights)
    for lo, hi in FFN_SPLITS:
        x = _ffn_group(x, h, merge_weights, lo, hi)
    if final:
        x = _rms_rows(x, EPS) * merge_weights[7][...]
    y_ref[...] = x.reshape(SEQ_TILE, tt, D_MODEL)

    osl_prev[...] = osl_cur[...]

    @pl.when((c == nt - 1) & (k < nb))
    def _():
        hs_ref[...] = hst[...]
        hl_ref[...] = lst[...]
        for j in range(CONV_W - 1):
            cc_ref[j] = xpad[j * SEQ_TILE:(j + 1) * SEQ_TILE, :]


def _mix_call(sc, x, oret, hs0, hl0, cc0, sp, w, tt, layer, final):
    b, l, _ = sc.shape
    nt = l // tt
    nb = (b // SEQ_TILE) * nt
    rows = tt * SEQ_TILE
    scanned = lambda k: jnp.minimum(k, nb - 1)
    merged = lambda k: jnp.maximum(k - 1, 0)
    blk = lambda f, width: pl.BlockSpec((SEQ_TILE, tt, width), lambda k: (f(k) // nt, f(k) % nt, 0))
    seq_spec = lambda width: pl.BlockSpec((SEQ_TILE, width), lambda k: (scanned(k) // nt, 0))
    cc_spec = pl.BlockSpec((CONV_W - 1, SEQ_TILE, D_LRU), lambda k: (0, scanned(k) // nt, 0))
    return pl.pallas_call(
        functools.partial(_mix_kernel, tt=tt, nt=nt, nb=nb, final=final),
        grid=(nb + 1,),
        in_specs=[
            blk(scanned, D_SCAN), blk(merged, D_MODEL), blk(merged, D_RET),
            seq_spec(2 * N_SSM), seq_spec(D_LRU), cc_spec,
            _const_spec((D_SSM, 2 * N_SSM)), _const_spec((2, N_SSM)), _const_spec((2 * N_SSM, D_SSM)),
            _const_spec((1, D_SSM)),
            _const_spec((CONV_W, D_LRU)), _const_spec((1, D_LRU)),
            _const_spec((D_LRU, D_LRU)), _const_spec((1, D_LRU)),
            _const_spec((D_LRU, D_LRU)), _const_spec((1, D_LRU)), _const_spec((1, D_LRU)),
            _layer_spec((D_SSM, 2 * D_SSM), layer),
            _layer_spec((1, D_MODEL), layer), _layer_spec((D_MODEL, D_MODEL), layer),
            _layer_spec((1, D_MODEL), layer),
            _layer_spec((D_MODEL, D_FFN), layer), _layer_spec((D_MODEL, D_FFN), layer),
            _layer_spec((D_FFN, D_MODEL), layer),
            _const_spec((1, D_MODEL)),
        ],
        out_specs=[
            blk(merged, D_MODEL),
            seq_spec(2 * N_SSM), seq_spec(D_LRU), cc_spec,
        ],
        out_shape=[
            jax.ShapeDtypeStruct((b, l, D_MODEL), F32),
            jax.ShapeDtypeStruct((b, 2 * N_SSM), F32),
            jax.ShapeDtypeStruct((b, D_LRU), F32),
            jax.ShapeDtypeStruct((CONV_W - 1, b, D_LRU), F32),
        ],
        scratch_shapes=[
            pltpu.VMEM((D_SCAN // LANES, rows, LANES), F32),
            pltpu.VMEM((rows, 2 * N_SSM), F32),
            pltpu.VMEM((SEQ_TILE, 2 * N_SSM), F32),
            pltpu.VMEM((rows + (CONV_W - 1) * SEQ_TILE, D_LRU), F32),
            pltpu.VMEM((rows, D_LRU), F32),
            pltpu.VMEM((rows, D_LRU), F32),
            pltpu.VMEM((SEQ_TILE, D_LRU), F32),
            pltpu.VMEM(((D_SSM + D_LRU) // LANES, rows, LANES), F32),
            pltpu.VMEM((rows, D_SSM + D_LRU), F32),
            pltpu.VMEM((rows, D_SSM + D_LRU), F32),
        ],
        compiler_params=pltpu.CompilerParams(
            dimension_semantics=("arbitrary",), vmem_limit_bytes=VMEM_LIMIT),
        name="mix",
    )(sc, x, oret, hs0, hl0, cc0, *sp,
      w['w_glu'], w['mix_scale'], w['w_out'], w['norm_ffn'], w['w_gate'], w['w_up'], w['w_down'], w['norm_final'])


def _scan_params(a_re, a_im, b_re, b_im, c_re, c_im, d_skip, log_dt,
                 conv_w, conv_b, w_a, b_a, w_x, b_x, lam):
    dt = jnp.exp(log_dt)[:, None]
    mag = jnp.exp(a_re * dt)
    lb_re = mag * jnp.cos(a_im * dt)
    lb_im = mag * jnp.sin(a_im * dt)
    den = a_re * a_re + a_im * a_im
    f_re = ((lb_re - 1.0) * a_re + lb_im * a_im) / den
    f_im = (lb_im * a_re - (lb_re - 1.0) * a_im) / den
    bb_re = f_re[..., None] * b_re - f_im[..., None] * b_im
    bb_im = f_re[..., None] * b_im + f_im[..., None] * b_re
    eye_g = jnp.eye(SSM_GROUPS, dtype=F32)
    bd_in = lambda t: jnp.einsum('gpc,gh->gchp', t, eye_g).reshape(D_SSM, N_SSM)
    bmat = jnp.concatenate([bd_in(bb_re), bd_in(bb_im)], axis=1).astype(BF16)
    bd_out = lambda t: jnp.einsum('gcp,gh->gphc', t, eye_g).reshape(N_SSM, D_SSM)
    cmat = jnp.concatenate([bd_out(c_re), bd_out(-c_im)], axis=0).astype(BF16)
    lam_bar = jnp.stack([lb_re.reshape(N_SSM), lb_im.reshape(N_SSM)])
    eye_b = jnp.eye(LRU_BLOCKS, dtype=F32)
    bd_lru = lambda t: jnp.einsum('hij,hk->hikj', t, eye_b).reshape(D_LRU, D_LRU).astype(BF16)
    row = lambda t: t.reshape(1, -1)
    return (bmat, lam_bar, cmat, row(d_skip),
            conv_w, row(conv_b), bd_lru(w_a), row(b_a), bd_lru(w_x), row(b_x),
            row(jax.nn.softplus(-lam)))


def _rotary_table(pos_offset, l):
    pos = pos_offset + jnp.arange(l, dtype=jnp.int32)
    inv_freq = ROPE_BASE ** (-jnp.arange(0, HEAD_DIM, 2, dtype=F32) / HEAD_DIM)
    ang = pos.astype(F32)[:, None] * inv_freq[None, :]
    cos, sin = jnp.cos(ang), jnp.sin(ang)
    return jnp.concatenate([cos, cos, -sin, sin], axis=-1)


def _trunk(x, pos_offset, s_ret, s_ssm, s_lru, c_conv, w, scan_params):
    b, l, _ = x.shape
    tb = min(RET_BLOCK, l)
    tt = min(SCAN_BLOCK, l)
    nseq = min(b, max(1, RET_ROWS // tb), RET_SEQS_MAX)
    assert l % tb == 0 and l % tt == 0 and b % SEQ_TILE == 0 and b % nseq == 0
    rot = _rotary_table(pos_offset, l)
    tables = _ret_tables(tb)
    depth = len(scan_params)
    xs = x
    new_ssm, new_lru, new_conv = [], [], []
    zero_ret = s_ret is None
    if zero_ret:
        s_ret = jnp.zeros((1, nseq, RET_HEADS, HEAD_DIM, HEAD_DIM), F32)
    ret_all = jnp.zeros((depth, b, RET_HEADS, HEAD_DIM, HEAD_DIM), F32)
    for i in range(depth):
        sc, oret, ret_all = _pr_call(xs, w['norm_mix'], w['w_in'], rot, s_ret, zero_ret, ret_all, tables, tb,
                                     nseq, i)
        hs0 = jnp.concatenate([s_ssm[i][..., 0].reshape(b, N_SSM), s_ssm[i][..., 1].reshape(b, N_SSM)], axis=1)
        xs, hs, hl, cc = _mix_call(sc, xs, oret, hs0, s_lru[i],
                                   jnp.swapaxes(c_conv[i], 0, 1), scan_params[i], w, tt, i,
                                   final=(i == depth - 1))
        new_ssm.append(jnp.stack([hs[:, :N_SSM].reshape(b, SSM_GROUPS, SSM_STATE),
                                  hs[:, N_SSM:].reshape(b, SSM_GROUPS, SSM_STATE)], axis=-1))
        new_lru.append(hl)
        new_conv.append(jnp.swapaxes(cc, 0, 1))
    return (xs, ret_all, jnp.stack(new_ssm),
            jnp.stack(new_lru), jnp.stack(new_conv))


def _prepare_weights(norm_mix, w_in, mix_scale, w_out, ssm_w_glu, norm_ffn, w_ffn_gate, w_ffn_up, w_ffn_down,
                     norm_final):
    depth = norm_mix.shape[0]
    rows = lambda t: t.reshape(depth, 1, D_MODEL)
    return {
        'norm_mix': rows(norm_mix), 'w_in': w_in.astype(BF16),
        'mix_scale': rows(mix_scale), 'w_out': w_out.astype(BF16), 'w_glu': ssm_w_glu.astype(BF16),
        'norm_ffn': rows(norm_ffn), 'w_gate': w_ffn_gate.astype(BF16), 'w_up': w_ffn_up.astype(BF16),
        'w_down': w_ffn_down.astype(BF16), 'norm_final': norm_final.reshape(1, D_MODEL),
    }


def kernel(x_prompt, x_sample, state_ret, state_ssm, state_lru, cache_conv, norm_mix, w_in, mix_scale, w_out, ssm_a_re, ssm_a_im, ssm_b_re, ssm_b_im, ssm_c_re, ssm_c_im, ssm_d, ssm_log_dt, ssm_w_glu, lru_conv_w, lru_conv_b, lru_w_a, lru_b_a, lru_w_x, lru_b_x, lru_lambda, norm_ffn, w_ffn_gate, w_ffn_up, w_ffn_down, norm_final):
    w = _prepare_weights(norm_mix, w_in, mix_scale, w_out, ssm_w_glu, norm_ffn, w_ffn_gate, w_ffn_up, w_ffn_down,
                         norm_final)
    depth = norm_mix.shape[0]
    scan_params = [
        _scan_params(ssm_a_re[i], ssm_a_im[i], ssm_b_re[i], ssm_b_im[i], ssm_c_re[i], ssm_c_im[i],
                     ssm_d[i], ssm_log_dt[i], lru_conv_w[i], lru_conv_b[i],
                     lru_w_a[i], lru_b_a[i], lru_w_x[i], lru_b_x[i], lru_lambda[i])
        for i in range(depth)]
    bp = x_prompt.shape[0]
    z_ssm = jnp.zeros((depth, bp, SSM_GROUPS, SSM_STATE, 2), F32)
    z_lru = jnp.zeros((depth, bp, D_LRU), F32)
    z_conv = jnp.zeros((depth, bp, CONV_W - 1, D_LRU), F32)
    out_p = _trunk(x_prompt, 0, None, z_ssm, z_lru, z_conv, w, scan_params)
    out_s = _trunk(x_sample, PAST_LEN, state_ret, state_ssm, state_lru, cache_conv, w, scan_params)
    return (out_p[0], out_s[0]) + tuple(out_p[1:]) + tuple(out_s[1:])
```

```python
import functools
import math

import jax
import jax.numpy as jnp
from jax import lax
from jax.experimental import pallas as pl
from jax.experimental.pallas import tpu as pltpu

F32 = jnp.float32
BF16 = jnp.bfloat16

D_MODEL = 1024
D_RET = 512
D_SSM = 256
D_LRU = 256
RET_HEADS = 4
HEAD_DIM = 128
RET_CHUNK = 64
ROPE_BASE = 10000.0
SSM_GROUP = 16
SSM_GROUPS = 16
SSM_STATE = 64
N_SSM = SSM_GROUPS * SSM_STATE
LRU_BLOCKS = 4
CONV_W = 4
LRU_C = 8.0
D_FFN = 2816
D_IN = 4 * D_RET + D_SSM + 2 * D_LRU
D_SCAN = D_SSM + 2 * D_LRU
EPS = 1e-6
GN_EPS = 1e-5

LANES = 128
SEQ_TILE = 8
PROJ_SEQS = 2
CAST_ROWS_UP = 32
CAST_ROWS_DOWN = 128
RET_BLOCK = 256
RET_ROWS = 1024
RET_SEQS_MAX = 8
PAST_LEN = 1024
SCAN_BLOCK = 64
FFN_SPLITS = ((0, 1536), (1536, D_FFN))
VMEM_LIMIT = 60 * 1024 * 1024


def _const_spec(shape):
    nd = len(shape)
    return pl.BlockSpec(shape, lambda *_: (0,) * nd, pipeline_mode=pl.Buffered(1))


def _layer_spec(shape, layer):
    nd = len(shape)
    return pl.BlockSpec((None,) + tuple(shape), lambda *_: (layer,) + (0,) * nd, pipeline_mode=pl.Buffered(1))


def _sigmoid(x):
    return 0.5 * (1.0 + jnp.tanh(0.5 * x))


def _gelu_tanh(x):
    return 0.5 * x * (1.0 + jnp.tanh(math.sqrt(2.0 / math.pi) * (x + 0.044715 * (x * x * x))))


def _rms_rows(x, eps):
    return x * lax.rsqrt(jnp.mean(x * x, axis=-1, keepdims=True) + eps)


def _proj_block(x, nw_ref, w_ref, rot_ref, qkv_out, g_out, sc_ref, nseq, tb):
    cos2 = rot_ref[:, :HEAD_DIM]
    sin2 = rot_ref[:, HEAD_DIM:]
    q_scale = HEAD_DIM ** -0.5
    h = (_rms_rows(x, EPS) * nw_ref[...]).astype(BF16)
    for j0 in range(0, nseq, PROJ_SEQS):
        j1 = min(j0 + PROJ_SEQS, nseq)
        proj = jnp.dot(h[j0 * tb:j1 * tb], w_ref[...], preferred_element_type=F32)
        for j in range(j0, j1):
            pj = proj[(j - j0) * tb:(j - j0 + 1) * tb]
            for head in range(2 * RET_HEADS):
                lo = head * HEAD_DIM
                xh = pj[:, lo:lo + HEAD_DIM]
                r = xh * cos2 + pltpu.roll(xh, HEAD_DIM // 2, axis=1) * sin2
                if head < RET_HEADS:
                    r = r * q_scale
                qkv_out[j, :, lo:lo + HEAD_DIM] = r.astype(BF16)
            qkv_out[j, :, 2 * D_RET:3 * D_RET] = pj[:, 2 * D_RET:3 * D_RET].astype(BF16)
            gg = pj[:, 3 * D_RET:4 * D_RET]
            g_out[j] = gg * _sigmoid(gg)
            sc_ref[j, :, :D_SSM + D_LRU] = pj[:, 4 * D_RET:4 * D_RET + D_SSM + D_LRU]
            sc_ref[j, :, D_SSM + D_LRU:] = _gelu_tanh(pj[:, 4 * D_RET + D_SSM + D_LRU:])


def _ret_block(qkv, g, tables, s_scr, o_ref, nseq):
    dmask_ref, qdec_ref, kdec_ref, gblk_ref = tables
    for j in range(nseq):
        for h in range(RET_HEADS):
            lo = h * HEAD_DIM
            q = qkv[j, :, lo:lo + HEAD_DIM]
            k = qkv[j, :, D_RET + lo:D_RET + lo + HEAD_DIM]
            v = qkv[j, :, 2 * D_RET + lo:2 * D_RET + lo + HEAD_DIM]
            s = s_scr[j, h]
            scores = lax.dot_general(q, k, (((1,), (1,)), ((), ())), preferred_element_type=F32)
            a = (scores * dmask_ref[h]).astype(BF16)
            o = jnp.dot(a, v, preferred_element_type=F32)
            o = o + jnp.dot(q, s.astype(BF16), preferred_element_type=F32) * qdec_ref[h]
            kd = (k.astype(F32) * kdec_ref[h]).astype(BF16)
            u = lax.dot_general(kd, v, (((0,), (0,)), ((), ())), preferred_element_type=F32)
            s_scr[j, h] = gblk_ref[h] * s + u
            oc = o - jnp.mean(o, axis=-1, keepdims=True)
            on = oc * lax.rsqrt(jnp.mean(oc * oc, axis=-1, keepdims=True) + GN_EPS)
            o_ref[j, :, lo:lo + HEAD_DIM] = on * g[j, :, lo:lo + HEAD_DIM]


def _pr_kernel(*refs, nseq, tb, nt, nb, cast_ffn):
    x_ref, nw_ref, w_ref, rot_ref, s0_ref, s_all_ref, dmask_ref, qdec_ref, kdec_ref, gblk_ref = refs[:10]
    refs = refs[10:]
    if cast_ffn:
        f32_refs, refs = refs[:3], refs[3:]
    sc_ref, o_ref, sfin_ref = refs[:3]
    refs = refs[3:]
    if cast_ffn:
        bf16_refs, refs = refs[:3], refs[3:]
        for src, dst in zip(f32_refs, bf16_refs):
            dst[...] = src[...].astype(BF16)
    s_scr, qkv_cur, qkv_prev, g_cur, g_prev = refs
    del s_all_ref
    k = pl.program_id(0)
    c_ret = lax.rem(jnp.maximum(k - 1, 0), nt)

    @pl.when(k == 0)
    def _():
        qkv_prev[...] = jnp.zeros(qkv_prev.shape, BF16)
        g_prev[...] = jnp.zeros(g_prev.shape, F32)

    @pl.when(c_ret == 0)
    def _():
        s_scr[...] = s0_ref[...]

    _proj_block(x_ref[...].reshape(nseq * tb, D_MODEL), nw_ref, w_ref, rot_ref, qkv_cur, g_cur, sc_ref, nseq, tb)
    _ret_block(qkv_prev, g_prev, (dmask_ref, qdec_ref, kdec_ref, gblk_ref), s_scr, o_ref, nseq)
    qkv_prev[...] = qkv_cur[...]
    g_prev[...] = g_cur[...]

    @pl.when((c_ret == nt - 1) & (k >= 1))
    def _():
        sfin_ref[...] = s_scr[...]


def _ret_tables(tb):
    h = jnp.arange(RET_HEADS, dtype=F32)
    gamma = 1.0 - 2.0 ** (-5.0 - h)
    log_g = jnp.log(gamma)
    idx = jnp.arange(tb, dtype=F32)
    chunk = jnp.arange(tb, dtype=jnp.int32) // RET_CHUNK
    diff = idx[:, None] - idx[None, :]
    same = chunk[:, None] == chunk[None, :]
    earlier = chunk[:, None] > chunk[None, :]
    expo = jnp.where(same, jnp.abs(diff), diff)
    dmask = jnp.where((same | earlier)[None], jnp.exp(log_g[:, None, None] * expo[None]), 0.0)
    qdec = jnp.exp(log_g[:, None] * (idx + 1.0)[None, :])
    kdec = jnp.exp(log_g[:, None] * (tb - 1.0 - idx)[None, :])
    gblk = jnp.exp(log_g * tb)
    wide = lambda t: jnp.broadcast_to(t[:, :, None], t.shape + (HEAD_DIM,))
    return dmask, wide(qdec), wide(kdec), jnp.broadcast_to(gblk[:, None, None], (RET_HEADS, 1, HEAD_DIM))


def _cast_steps():
    return max(D_MODEL // CAST_ROWS_UP, D_FFN // CAST_ROWS_DOWN)


def _pr_call(x, nw, w_in, rot, s0, s0_shared, s_all, tables, tb, nseq, layer, ffn_f32=None):
    b, l, _ = x.shape
    nt = l // tb
    nb = (b // nseq) * nt
    cast_ffn = ffn_f32 is not None
    cast_in, cast_out, cast_shapes = [], [], []
    if cast_ffn:
        assert nb + 1 >= _cast_steps() and D_MODEL % CAST_ROWS_UP == 0 and D_FFN % CAST_ROWS_DOWN == 0
        for rows, shape in ((CAST_ROWS_UP, (D_MODEL, D_FFN)), (CAST_ROWS_UP, (D_MODEL, D_FFN)),
                            (CAST_ROWS_DOWN, (D_FFN, D_MODEL))):
            last = shape[0] // rows - 1
            cast_in.append(pl.BlockSpec((None, rows, shape[1]),
                                        lambda k, last=last: (layer, jnp.minimum(k, last), 0)))
            cast_out.append(pl.BlockSpec((rows, shape[1]), lambda k, last=last: (jnp.minimum(k, last), 0)))
            cast_shapes.append(jax.ShapeDtypeStruct(shape, BF16))
    dmask, qdec, kdec, gblk = tables
    projected = lambda k: jnp.minimum(k, nb - 1)
    retained = lambda k: jnp.maximum(k - 1, 0)
    blk = lambda f, width: pl.BlockSpec((nseq, tb, width), lambda k: (f(k) // nt, f(k) % nt, 0))
    state_block = (None, nseq, RET_HEADS, HEAD_DIM, HEAD_DIM)
    if s0_shared:
        s0_spec = pl.BlockSpec(state_block, lambda k: (0, 0, 0, 0, 0))
    else:
        s0_spec = pl.BlockSpec(state_block, lambda k: (layer, retained(k) // nt, 0, 0, 0))
    outs = pl.pallas_call(
        functools.partial(_pr_kernel, nseq=nseq, tb=tb, nt=nt, nb=nb, cast_ffn=cast_ffn),
        grid=(nb + 1,),
        in_specs=[
            blk(projected, D_MODEL),
            _layer_spec((1, D_MODEL), layer),
            _layer_spec((D_MODEL, D_IN), layer),
            pl.BlockSpec((tb, 2 * HEAD_DIM), lambda k: (projected(k) % nt, 0)),
            s0_spec,
            pl.BlockSpec(memory_space=pl.ANY),
            _const_spec((RET_HEADS, tb, tb)),
            _const_spec((RET_HEADS, tb, HEAD_DIM)),
            _const_spec((RET_HEADS, tb, HEAD_DIM)),
            _const_spec((RET_HEADS, 1, HEAD_DIM)),
        ] + cast_in,
        out_specs=[
            blk(projected, D_SCAN),
            blk(retained, D_RET),
            pl.BlockSpec(state_block, lambda k: (layer, retained(k) // nt, 0, 0, 0)),
        ] + cast_out,
        out_shape=[
            jax.ShapeDtypeStruct((b, l, D_SCAN), F32),
            jax.ShapeDtypeStruct((b, l, D_RET), F32),
            jax.ShapeDtypeStruct(s_all.shape, F32),
        ] + cast_shapes,
        input_output_aliases={5: 2},
        scratch_shapes=[
            pltpu.VMEM((nseq, RET_HEADS, HEAD_DIM, HEAD_DIM), F32),
            pltpu.VMEM((nseq, tb, 3 * D_RET), BF16),
            pltpu.VMEM((nseq, tb, 3 * D_RET), BF16),
            pltpu.VMEM((nseq, tb, D_RET), F32),
            pltpu.VMEM((nseq, tb, D_RET), F32),
        ],
        compiler_params=pltpu.CompilerParams(
            dimension_semantics=("arbitrary",), vmem_limit_bytes=VMEM_LIMIT),
        name="pr",
    )(x, nw, w_in, rot, s0, s_all, dmask, qdec, kdec, gblk, *(ffn_f32 or ()))
    return outs[:3], tuple(outs[3:])


def _scan_pre(sc_ref, params, scratch, tt):
    bmat_ref, lam_ref, cmat_ref, dskip_ref, convw_ref, convb_ref, wa_ref, ba_ref, wx_ref, bxb_ref, sp_ref = params
    tm, hbuf, hst, xpad, abuf, bbuf, lst, res = scratch
    rows = tt * SEQ_TILE
    pad_rows = (CONV_W - 1) * SEQ_TILE

    for s in range(SEQ_TILE):
        for j in range(D_SCAN // LANES):
            tm[j, pl.ds(s, tt, stride=SEQ_TILE), :] = sc_ref[s, :, j * LANES:(j + 1) * LANES]

    u = jnp.concatenate([tm[0], tm[1]], axis=-1)
    hbuf[...] = jnp.dot(u.astype(BF16), bmat_ref[...], preferred_element_type=F32)

    xpad[pad_rows:, :] = jnp.concatenate([tm[2], tm[3]], axis=-1)
    xc = convb_ref[...] + xpad[0:rows, :] * convw_ref[0:1, :]
    for j in range(1, CONV_W):
        xc = xc + xpad[j * SEQ_TILE:j * SEQ_TILE + rows, :] * convw_ref[j:j + 1, :]
    xcb = xc.astype(BF16)
    r = _sigmoid(jnp.dot(xcb, wa_ref[...], preferred_element_type=F32) + ba_ref[...])
    ig = _sigmoid(jnp.dot(xcb, wx_ref[...], preferred_element_type=F32) + bxb_ref[...])
    log_a = (-LRU_C) * r * sp_ref[...]
    a = jnp.exp(log_a)
    abuf[...] = a
    t1 = 1.0 - a * a
    bbuf[...] = jnp.where(t1 > 0.0, t1 * lax.rsqrt(t1), 0.0) * (ig * xc)

    return hst[:, 0:N_SSM], hst[:, N_SSM:2 * N_SSM], lst[...]


def _scan_steps(params, scratch, carry, t0, t1):
    lam_ref = params[1]
    tm, hbuf, hst, xpad, abuf, bbuf, lst, res = scratch
    lam_re = lam_ref[0:1, :]
    lam_im = lam_ref[1:2, :]
    hr, hi, hl = carry
    for t in range(t0, t1):
        lo = t * SEQ_TILE
        br = hbuf[lo:lo + SEQ_TILE, 0:N_SSM]
        bi = hbuf[lo:lo + SEQ_TILE, N_SSM:2 * N_SSM]
        hr, hi = lam_re * hr - lam_im * hi + br, lam_re * hi + lam_im * hr + bi
        hbuf[lo:lo + SEQ_TILE, 0:N_SSM] = hr
        hbuf[lo:lo + SEQ_TILE, N_SSM:2 * N_SSM] = hi
        hl = abuf[lo:lo + SEQ_TILE, :] * hl + bbuf[lo:lo + SEQ_TILE, :]
        bbuf[lo:lo + SEQ_TILE, :] = hl
    return hr, hi, hl


def _scan_post(params, scratch, carry, osl_out, tt):
    bmat_ref, lam_ref, cmat_ref, dskip_ref = params[:4]
    tm, hbuf, hst, xpad, abuf, bbuf, lst, res = scratch
    rows = tt * SEQ_TILE
    hr, hi, hl = carry
    hst[:, 0:N_SSM] = hr
    hst[:, N_SSM:2 * N_SSM] = hi
    lst[...] = hl
    for j in range(CONV_W - 1):
        xpad[j * SEQ_TILE:(j + 1) * SEQ_TILE, :] = xpad[rows + j * SEQ_TILE:rows + (j + 1) * SEQ_TILE, :]

    u = jnp.concatenate([tm[0], tm[1]], axis=-1)
    y = jnp.dot(hbuf[...].astype(BF16), cmat_ref[...], preferred_element_type=F32) + dskip_ref[...] * u
    o_lru = bbuf[...] * jnp.concatenate([tm[4], tm[5]], axis=-1)
    res[0] = y[:, :LANES]
    res[1] = y[:, LANES:]
    res[2] = o_lru[:, :LANES]
    res[3] = o_lru[:, LANES:]
    for s in range(SEQ_TILE):
        for j in range((D_SSM + D_LRU) // LANES):
            osl_out[s * tt:(s + 1) * tt, j * LANES:(j + 1) * LANES] = res[j, pl.ds(s, tt, stride=SEQ_TILE), :]


def _merge_head(x, oret, osl, weights):
    wglu_ref, ms_ref, wo_ref, nf_ref = weights[:4]
    ga = jnp.dot(_gelu_tanh(osl[:, :D_SSM]).astype(BF16), wglu_ref[...], preferred_element_type=F32)
    o_ssm = ga[:, :D_SSM] * _sigmoid(ga[:, D_SSM:])
    mixed = jnp.concatenate(
        [_rms_rows(oret, EPS), _rms_rows(o_ssm, EPS), _rms_rows(osl[:, D_SSM:], EPS)], axis=-1) * ms_ref[...]
    x = x + jnp.dot(mixed.astype(BF16), wo_ref[...], preferred_element_type=F32)
    return x, (_rms_rows(x, EPS) * nf_ref[...]).astype(BF16)


def _ffn_group(x, h, weights, lo, hi):
    wg_ref, wu_ref, wd_ref = weights[4:7]
    gate = jnp.dot(h, wg_ref[:, lo:hi], preferred_element_type=F32)
    up = jnp.dot(h, wu_ref[:, lo:hi], preferred_element_type=F32)
    ff = (gate * _sigmoid(gate) * up).astype(BF16)
    return x + jnp.dot(ff, wd_ref[lo:hi, :], preferred_element_type=F32)


def _mix_kernel(*refs, tt, nt, nb, final):
    sc_ref, x_ref, oret_ref, hs0_ref, hl0_ref, cc0_ref = refs[:6]
    scan_params = refs[6:17]
    merge_weights = refs[17:25]
    y_ref, hs_ref, hl_ref, cc_ref = refs[25:29]
    scratch = refs[29:37]
    osl_cur, osl_prev = refs[37:39]
    tm, hbuf, hst, xpad, abuf, bbuf, lst, res = scratch
    k = pl.program_id(0)
    c = lax.rem(k, nt)
    rows = tt * SEQ_TILE

    @pl.when(k == 0)
    def _():
        osl_prev[...] = jnp.zeros(osl_prev.shape, F32)

    @pl.when((c == 0) & (k < nb))
    def _():
        hst[...] = hs0_ref[...]
        lst[...] = hl0_ref[...]
        for j in range(CONV_W - 1):
            xpad[j * SEQ_TILE:(j + 1) * SEQ_TILE, :] = cc0_ref[j]

    carry = _scan_pre(sc_ref, scan_params, scratch, tt)
    carry = _scan_steps(scan_params, scratch, carry, 0, tt)
    _scan_post(scan_params, scratch, carry, osl_cur, tt)

    x, h = _merge_head(x_ref[...].reshape(rows, D_MODEL), oret_ref[...].reshape(rows, D_RET),
                       osl_prev[...], merge_weights)
    for lo, hi in FFN_SPLITS:
        x = _ffn_group(x, h, merge_weights, lo, hi)
    if final:
        x = _rms_rows(x, EPS) * merge_weights[7][...]
    y_ref[...] = x.reshape(SEQ_TILE, tt, D_MODEL)

    osl_prev[...] = osl_cur[...]

    @pl.when((c == nt - 1) & (k < nb))
    def _():
        hs_ref[...] = hst[...]
        hl_ref[...] = lst[...]
        for j in range(CONV_W - 1):
            cc_ref[j] = xpad[j * SEQ_TILE:(j + 1) * SEQ_TILE, :]


def _mix_call(sc, x, oret, hs0, hl0, cc0, sp, w, ffn, tt, layer, final):
    b, l, _ = sc.shape
    nt = l // tt
    nb = (b // SEQ_TILE) * nt
    rows = tt * SEQ_TILE
    scanned = lambda k: jnp.minimum(k, nb - 1)
    merged = lambda k: jnp.maximum(k - 1, 0)
    blk = lambda f, width: pl.BlockSpec((SEQ_TILE, tt, width), lambda k: (f(k) // nt, f(k) % nt, 0))
    seq_spec = lambda width: pl.BlockSpec((SEQ_TILE, width), lambda k: (scanned(k) // nt, 0))
    cc_spec = pl.BlockSpec((CONV_W - 1, SEQ_TILE, D_LRU), lambda k: (0, scanned(k) // nt, 0))
    seq0_spec = lambda width: pl.BlockSpec((None, SEQ_TILE, width), lambda k: (layer, scanned(k) // nt, 0))
    cc0_spec = pl.BlockSpec((None, CONV_W - 1, SEQ_TILE, D_LRU), lambda k: (layer, 0, scanned(k) // nt, 0))
    return pl.pallas_call(
        functools.partial(_mix_kernel, tt=tt, nt=nt, nb=nb, final=final),
        grid=(nb + 1,),
        in_specs=[
            blk(scanned, D_SCAN), blk(merged, D_MODEL), blk(merged, D_RET),
            seq0_spec(2 * N_SSM), seq0_spec(D_LRU), cc0_spec,
        ] + [_layer_spec(t.shape[1:], layer) for t in sp] + [
            _layer_spec((D_SSM, 2 * D_SSM), layer),
            _layer_spec((1, D_MODEL), layer), _layer_spec((D_MODEL, D_MODEL), layer),
            _layer_spec((1, D_MODEL), layer),
            _const_spec((D_MODEL, D_FFN)), _const_spec((D_MODEL, D_FFN)), _const_spec((D_FFN, D_MODEL)),
            _const_spec((1, D_MODEL)),
        ],
        out_specs=[
            blk(merged, D_MODEL),
            seq_spec(2 * N_SSM), seq_spec(D_LRU), cc_spec,
        ],
        out_shape=[
            jax.ShapeDtypeStruct((b, l, D_MODEL), F32),
            jax.ShapeDtypeStruct((b, 2 * N_SSM), F32),
            jax.ShapeDtypeStruct((b, D_LRU), F32),
            jax.ShapeDtypeStruct((CONV_W - 1, b, D_LRU), F32),
        ],
        scratch_shapes=[
            pltpu.VMEM((D_SCAN // LANES, rows, LANES), F32),
            pltpu.VMEM((rows, 2 * N_SSM), F32),
            pltpu.VMEM((SEQ_TILE, 2 * N_SSM), F32),
            pltpu.VMEM((rows + (CONV_W - 1) * SEQ_TILE, D_LRU), F32),
            pltpu.VMEM((rows, D_LRU), F32),
            pltpu.VMEM((rows, D_LRU), F32),
            pltpu.VMEM((SEQ_TILE, D_LRU), F32),
            pltpu.VMEM(((D_SSM + D_LRU) // LANES, rows, LANES), F32),
            pltpu.VMEM((rows, D_SSM + D_LRU), F32),
            pltpu.VMEM((rows, D_SSM + D_LRU), F32),
        ],
        compiler_params=pltpu.CompilerParams(
            dimension_semantics=("arbitrary",), vmem_limit_bytes=VMEM_LIMIT),
        name="mix",
    )(sc, x, oret, hs0, hl0, cc0, *sp,
      w['w_glu'], w['mix_scale'], w['w_out'], w['norm_ffn'], *ffn, w['norm_final'])


def _scan_params(a_re, a_im, b_re, b_im, c_re, c_im, d_skip, log_dt,
                 conv_w, conv_b, w_a, b_a, w_x, b_x, lam):
    dt = jnp.exp(log_dt)[:, None]
    mag = jnp.exp(a_re * dt)
    lb_re = mag * jnp.cos(a_im * dt)
    lb_im = mag * jnp.sin(a_im * dt)
    den = a_re * a_re + a_im * a_im
    f_re = ((lb_re - 1.0) * a_re + lb_im * a_im) / den
    f_im = (lb_im * a_re - (lb_re - 1.0) * a_im) / den
    bb_re = f_re[..., None] * b_re - f_im[..., None] * b_im
    bb_im = f_re[..., None] * b_im + f_im[..., None] * b_re
    eye_g = jnp.eye(SSM_GROUPS, dtype=F32)
    bd_in = lambda t: jnp.einsum('gpc,gh->gchp', t, eye_g).reshape(D_SSM, N_SSM)
    bmat = jnp.concatenate([bd_in(bb_re), bd_in(bb_im)], axis=1).astype(BF16)
    bd_out = lambda t: jnp.einsum('gcp,gh->gphc', t, eye_g).reshape(N_SSM, D_SSM)
    cmat = jnp.concatenate([bd_out(c_re), bd_out(-c_im)], axis=0).astype(BF16)
    lam_bar = jnp.stack([lb_re.reshape(N_SSM), lb_im.reshape(N_SSM)])
    eye_b = jnp.eye(LRU_BLOCKS, dtype=F32)
    bd_lru = lambda t: jnp.einsum('hij,hk->hikj', t, eye_b).reshape(D_LRU, D_LRU).astype(BF16)
    row = lambda t: t.reshape(1, -1)
    return (bmat, lam_bar, cmat, row(d_skip),
            conv_w, row(conv_b), bd_lru(w_a), row(b_a), bd_lru(w_x), row(b_x),
            row(jax.nn.softplus(-lam)))


def _rotary_table(pos_offset, l):
    pos = pos_offset + jnp.arange(l, dtype=jnp.int32)
    inv_freq = ROPE_BASE ** (-jnp.arange(0, HEAD_DIM, 2, dtype=F32) / HEAD_DIM)
    ang = pos.astype(F32)[:, None] * inv_freq[None, :]
    cos, sin = jnp.cos(ang), jnp.sin(ang)
    return jnp.concatenate([cos, cos, -sin, sin], axis=-1)


def _trunk(x, pos_offset, s_ret, s_ssm, s_lru, c_conv, w, sp, ffn_f32, ffn_bf16):
    b, l, _ = x.shape
    depth = s_ssm.shape[0]
    tb = min(RET_BLOCK, l)
    tt = min(SCAN_BLOCK, l)
    nseq = min(b, max(1, RET_ROWS // tb), RET_SEQS_MAX)
    assert l % tb == 0 and l % tt == 0 and b % SEQ_TILE == 0 and b % nseq == 0
    rot = _rotary_table(pos_offset, l)
    tables = _ret_tables(tb)
    cast_in_pr = ffn_bf16 is None and (b // nseq) * (l // tb) + 1 >= _cast_steps()
    if ffn_bf16 is None and not cast_in_pr:
        ffn_bf16 = [tuple(t[i].astype(BF16) for t in ffn_f32) for i in range(depth)]
    zero_ret = s_ret is None
    if zero_ret:
        s_ret = jnp.zeros((1, nseq, RET_HEADS, HEAD_DIM, HEAD_DIM), F32)
    ret_all = jnp.zeros((depth, b, RET_HEADS, HEAD_DIM, HEAD_DIM), F32)
    hs0 = jnp.concatenate([s_ssm[..., 0].reshape(depth, b, N_SSM), s_ssm[..., 1].reshape(depth, b, N_SSM)], axis=-1)
    cc0 = jnp.swapaxes(c_conv, 1, 2)
    xs = x
    ffn_out, new_hs, new_lru, new_cc = [], [], [], []
    for i in range(depth):
        (sc, oret, ret_all), cast = _pr_call(xs, w['norm_mix'], w['w_in'], rot, s_ret, zero_ret, ret_all, tables,
                                             tb, nseq, i, ffn_f32 if cast_in_pr else None)
        ffn_out.append(cast if cast_in_pr else ffn_bf16[i])
        xs, hs, hl, cc = _mix_call(sc, xs, oret, hs0, s_lru, cc0, sp, w, ffn_out[i], tt, i,
                                   final=(i == depth - 1))
        new_hs.append(hs)
        new_lru.append(hl)
        new_cc.append(cc)
    hs = jnp.stack(new_hs)
    new_ssm = jnp.stack([hs[..., :N_SSM].reshape(depth, b, SSM_GROUPS, SSM_STATE),
                         hs[..., N_SSM:].reshape(depth, b, SSM_GROUPS, SSM_STATE)], axis=-1)
    return (xs, ret_all, new_ssm, jnp.stack(new_lru), jnp.swapaxes(jnp.stack(new_cc), 1, 2)), ffn_out


def _prepare_weights(norm_mix, w_in, mix_scale, w_out, ssm_w_glu, norm_ffn, norm_final):
    depth = norm_mix.shape[0]
    rows = lambda t: t.reshape(depth, 1, D_MODEL)
    return {
        'norm_mix': rows(norm_mix), 'w_in': w_in.astype(BF16),
        'mix_scale': rows(mix_scale), 'w_out': w_out.astype(BF16), 'w_glu': ssm_w_glu.astype(BF16),
        'norm_ffn': rows(norm_ffn), 'norm_final': norm_final.reshape(1, D_MODEL),
    }


def kernel(x_prompt, x_sample, state_ret, state_ssm, state_lru, cache_conv, norm_mix, w_in, mix_scale, w_out, ssm_a_re, ssm_a_im, ssm_b_re, ssm_b_im, ssm_c_re, ssm_c_im, ssm_d, ssm_log_dt, ssm_w_glu, lru_conv_w, lru_conv_b, lru_w_a, lru_b_a, lru_w_x, lru_b_x, lru_lambda, norm_ffn, w_ffn_gate, w_ffn_up, w_ffn_down, norm_final):
    w = _prepare_weights(norm_mix, w_in, mix_scale, w_out, ssm_w_glu, norm_ffn, norm_final)
    depth = norm_mix.shape[0]
    sp = jax.vmap(_scan_params)(ssm_a_re, ssm_a_im, ssm_b_re, ssm_b_im, ssm_c_re, ssm_c_im, ssm_d, ssm_log_dt,
                                lru_conv_w, lru_conv_b, lru_w_a, lru_b_a, lru_w_x, lru_b_x, lru_lambda)
    ffn_f32 = (w_ffn_gate, w_ffn_up, w_ffn_down)
    bp = x_prompt.shape[0]
    z_ssm = jnp.zeros((depth, bp, SSM_GROUPS, SSM_STATE, 2), F32)
    z_lru = jnp.zeros((depth, bp, D_LRU), F32)
    z_conv = jnp.zeros((depth, bp, CONV_W - 1, D_LRU), F32)
    out_p, ffn_bf16 = _trunk(x_prompt, 0, None, z_ssm, z_lru, z_conv, w, sp, ffn_f32, None)
    out_s, _ = _trunk(x_sample, PAST_LEN, state_ret, state_ssm, state_lru, cache_conv, w, sp, ffn_f32, ffn_bf16)
    return (out_p[0], out_s[0]) + tuple(out_p[1:]) + tuple(out_s[1:])
```

```python
import functools
import math

import jax
import jax.numpy as jnp
from jax import lax
from jax.experimental import pallas as pl
from jax.experimental.pallas import tpu as pltpu

F32 = jnp.float32
BF16 = jnp.bfloat16

D_MODEL = 1024
D_RET = 512
D_SSM = 256
D_LRU = 256
RET_HEADS = 4
HEAD_DIM = 128
RET_CHUNK = 64
ROPE_BASE = 10000.0
SSM_GROUP = 16
SSM_GROUPS = 16
SSM_STATE = 64
N_SSM = SSM_GROUPS * SSM_STATE
LRU_BLOCKS = 4
CONV_W = 4
LRU_C = 8.0
D_FFN = 2816
D_IN = 4 * D_RET + D_SSM + 2 * D_LRU
D_SCAN = D_SSM + 2 * D_LRU
EPS = 1e-6
GN_EPS = 1e-5

LANES = 128
SEQ_TILE = 8
PROJ_SEQS = 2
CAST_ROWS_UP = 32
CAST_ROWS_DOWN = 128
RET_BLOCK = 256
RET_ROWS = 1024
RET_SEQS_MAX = 8
PAST_LEN = 1024
SCAN_BLOCK = 64
FFN_SPLITS = ((0, 1536), (1536, D_FFN))
VMEM_LIMIT = 60 * 1024 * 1024


def _const_spec(shape):
    nd = len(shape)
    return pl.BlockSpec(shape, lambda *_: (0,) * nd, pipeline_mode=pl.Buffered(1))


def _layer_spec(shape, layer):
    nd = len(shape)
    return pl.BlockSpec((None,) + tuple(shape), lambda *_: (layer,) + (0,) * nd, pipeline_mode=pl.Buffered(1))


def _sigmoid(x):
    return 0.5 * (1.0 + jnp.tanh(0.5 * x))


def _gelu_tanh(x):
    return 0.5 * x * (1.0 + jnp.tanh(math.sqrt(2.0 / math.pi) * (x + 0.044715 * (x * x * x))))


def _rms_rows(x, eps):
    return x * lax.rsqrt(jnp.mean(x * x, axis=-1, keepdims=True) + eps)


def _proj_block(x, nw_ref, w_ref, rot_ref, qkv_out, g_out, sc_ref, nseq, tb):
    cos2 = rot_ref[:, :HEAD_DIM]
    sin2 = rot_ref[:, HEAD_DIM:]
    q_scale = HEAD_DIM ** -0.5
    h = (_rms_rows(x, EPS) * nw_ref[...]).astype(BF16)
    for j0 in range(0, nseq, PROJ_SEQS):
        j1 = min(j0 + PROJ_SEQS, nseq)
        proj = jnp.dot(h[j0 * tb:j1 * tb], w_ref[...], preferred_element_type=F32)
        for j in range(j0, j1):
            pj = proj[(j - j0) * tb:(j - j0 + 1) * tb]
            for head in range(2 * RET_HEADS):
                lo = head * HEAD_DIM
                xh = pj[:, lo:lo + HEAD_DIM]
                r = xh * cos2 + pltpu.roll(xh, HEAD_DIM // 2, axis=1) * sin2
                if head < RET_HEADS:
                    r = r * q_scale
                qkv_out[j, :, lo:lo + HEAD_DIM] = r.astype(BF16)
            qkv_out[j, :, 2 * D_RET:3 * D_RET] = pj[:, 2 * D_RET:3 * D_RET].astype(BF16)
            gg = pj[:, 3 * D_RET:4 * D_RET]
            g_out[j] = gg * _sigmoid(gg)
            sc_ref[j, :, :D_SSM + D_LRU] = pj[:, 4 * D_RET:4 * D_RET + D_SSM + D_LRU]
            sc_ref[j, :, D_SSM + D_LRU:] = _gelu_tanh(pj[:, 4 * D_RET + D_SSM + D_LRU:])


def _ret_block(qkv, g, tables, s_scr, o_ref, nseq):
    dmask_ref, qdec_ref, kdec_ref, gblk_ref = tables
    for j in range(nseq):
        for h in range(RET_HEADS):
            lo = h * HEAD_DIM
            q = qkv[j, :, lo:lo + HEAD_DIM]
            k = qkv[j, :, D_RET + lo:D_RET + lo + HEAD_DIM]
            v = qkv[j, :, 2 * D_RET + lo:2 * D_RET + lo + HEAD_DIM]
            s = s_scr[j, h]
            scores = lax.dot_general(q, k, (((1,), (1,)), ((), ())), preferred_element_type=F32)
            a = (scores * dmask_ref[h]).astype(BF16)
            o = jnp.dot(a, v, preferred_element_type=F32)
            o = o + jnp.dot(q, s.astype(BF16), preferred_element_type=F32) * qdec_ref[h]
            kd = (k.astype(F32) * kdec_ref[h]).astype(BF16)
            u = lax.dot_general(kd, v, (((0,), (0,)), ((), ())), preferred_element_type=F32)
            s_scr[j, h] = gblk_ref[h] * s + u
            oc = o - jnp.mean(o, axis=-1, keepdims=True)
            on = oc * lax.rsqrt(jnp.mean(oc * oc, axis=-1, keepdims=True) + GN_EPS)
            o_ref[j, :, lo:lo + HEAD_DIM] = on * g[j, :, lo:lo + HEAD_DIM]


def _pr_kernel(*refs, nseq, tb, nt, nb, n_cast):
    x_ref, nw_ref, w_ref, rot_ref, s0_ref, s_all_ref, dmask_ref, qdec_ref, kdec_ref, gblk_ref = refs[:10]
    f32_refs = refs[10:10 + n_cast]
    sc_ref, o_ref, sfin_ref = refs[10 + n_cast:13 + n_cast]
    bf16_refs = refs[13 + n_cast:13 + 2 * n_cast]
    s_scr, qkv_cur, qkv_prev, g_cur, g_prev = refs[13 + 2 * n_cast:]
    for src, dst in zip(f32_refs, bf16_refs):
        dst[...] = src[...].astype(BF16)
    del s_all_ref
    k = pl.program_id(0)
    c_ret = lax.rem(jnp.maximum(k - 1, 0), nt)

    @pl.when(k == 0)
    def _():
        qkv_prev[...] = jnp.zeros(qkv_prev.shape, BF16)
        g_prev[...] = jnp.zeros(g_prev.shape, F32)

    @pl.when(c_ret == 0)
    def _():
        s_scr[...] = s0_ref[...]

    _proj_block(x_ref[...].reshape(nseq * tb, D_MODEL), nw_ref, w_ref, rot_ref, qkv_cur, g_cur, sc_ref, nseq, tb)
    _ret_block(qkv_prev, g_prev, (dmask_ref, qdec_ref, kdec_ref, gblk_ref), s_scr, o_ref, nseq)
    qkv_prev[...] = qkv_cur[...]
    g_prev[...] = g_cur[...]

    @pl.when((c_ret == nt - 1) & (k >= 1))
    def _():
        sfin_ref[...] = s_scr[...]


def _ret_tables(tb):
    h = jnp.arange(RET_HEADS, dtype=F32)
    gamma = 1.0 - 2.0 ** (-5.0 - h)
    log_g = jnp.log(gamma)
    idx = jnp.arange(tb, dtype=F32)
    chunk = jnp.arange(tb, dtype=jnp.int32) // RET_CHUNK
    diff = idx[:, None] - idx[None, :]
    same = chunk[:, None] == chunk[None, :]
    earlier = chunk[:, None] > chunk[None, :]
    expo = jnp.where(same, jnp.abs(diff), diff)
    dmask = jnp.where((same | earlier)[None], jnp.exp(log_g[:, None, None] * expo[None]), 0.0)
    qdec = jnp.exp(log_g[:, None] * (idx + 1.0)[None, :])
    kdec = jnp.exp(log_g[:, None] * (tb - 1.0 - idx)[None, :])
    gblk = jnp.exp(log_g * tb)
    wide = lambda t: jnp.broadcast_to(t[:, :, None], t.shape + (HEAD_DIM,))
    return dmask, wide(qdec), wide(kdec), jnp.broadcast_to(gblk[:, None, None], (RET_HEADS, 1, HEAD_DIM))


def _cast_steps():
    return max(D_MODEL // CAST_ROWS_UP, D_FFN // CAST_ROWS_DOWN)


def _cast_jobs(big_f32, layer):
    w_in, w_out, w_gate, w_up, w_down = big_f32
    jobs = [(w_out, layer, CAST_ROWS_UP), (w_gate, layer, CAST_ROWS_UP), (w_up, layer, CAST_ROWS_UP),
            (w_down, layer, CAST_ROWS_DOWN)]
    if layer + 1 < w_in.shape[0]:
        jobs.append((w_in, layer + 1, CAST_ROWS_UP))
    return jobs


def _pr_call(x, nw, w_in, rot, s0, s0_shared, s_all, tables, tb, nseq, layer, cast_jobs=()):
    b, l, _ = x.shape
    nt = l // tb
    nb = (b // nseq) * nt
    cast_in, cast_out, cast_shapes = [], [], []
    for arr, lyr, rows in cast_jobs:
        shape = arr.shape[1:]
        assert shape[0] % rows == 0 and nb + 1 >= shape[0] // rows
        last = shape[0] // rows - 1
        cast_in.append(pl.BlockSpec((None, rows, shape[1]),
                                    lambda k, last=last, lyr=lyr: (lyr, jnp.minimum(k, last), 0)))
        cast_out.append(pl.BlockSpec((rows, shape[1]), lambda k, last=last: (jnp.minimum(k, last), 0)))
        cast_shapes.append(jax.ShapeDtypeStruct(shape, BF16))
    dmask, qdec, kdec, gblk = tables
    projected = lambda k: jnp.minimum(k, nb - 1)
    retained = lambda k: jnp.maximum(k - 1, 0)
    blk = lambda f, width: pl.BlockSpec((nseq, tb, width), lambda k: (f(k) // nt, f(k) % nt, 0))
    state_block = (None, nseq, RET_HEADS, HEAD_DIM, HEAD_DIM)
    if s0_shared:
        s0_spec = pl.BlockSpec(state_block, lambda k: (0, 0, 0, 0, 0))
    else:
        s0_spec = pl.BlockSpec(state_block, lambda k: (layer, retained(k) // nt, 0, 0, 0))
    outs = pl.pallas_call(
        functools.partial(_pr_kernel, nseq=nseq, tb=tb, nt=nt, nb=nb, n_cast=len(cast_jobs)),
        grid=(nb + 1,),
        in_specs=[
            blk(projected, D_MODEL),
            _layer_spec((1, D_MODEL), layer),
            _const_spec((D_MODEL, D_IN)),
            pl.BlockSpec((tb, 2 * HEAD_DIM), lambda k: (projected(k) % nt, 0)),
            s0_spec,
            pl.BlockSpec(memory_space=pl.ANY),
            _const_spec((RET_HEADS, tb, tb)),
            _const_spec((RET_HEADS, tb, HEAD_DIM)),
            _const_spec((RET_HEADS, tb, HEAD_DIM)),
            _const_spec((RET_HEADS, 1, HEAD_DIM)),
        ] + cast_in,
        out_specs=[
            blk(projected, D_SCAN),
            blk(retained, D_RET),
            pl.BlockSpec(state_block, lambda k: (layer, retained(k) // nt, 0, 0, 0)),
        ] + cast_out,
        out_shape=[
            jax.ShapeDtypeStruct((b, l, D_SCAN), F32),
            jax.ShapeDtypeStruct((b, l, D_RET), F32),
            jax.ShapeDtypeStruct(s_all.shape, F32),
        ] + cast_shapes,
        input_output_aliases={5: 2},
        scratch_shapes=[
            pltpu.VMEM((nseq, RET_HEADS, HEAD_DIM, HEAD_DIM), F32),
            pltpu.VMEM((nseq, tb, 3 * D_RET), BF16),
            pltpu.VMEM((nseq, tb, 3 * D_RET), BF16),
            pltpu.VMEM((nseq, tb, D_RET), F32),
            pltpu.VMEM((nseq, tb, D_RET), F32),
        ],
        compiler_params=pltpu.CompilerParams(
            dimension_semantics=("arbitrary",), vmem_limit_bytes=VMEM_LIMIT),
        name="pr",
    )(x, nw, w_in, rot, s0, s_all, dmask, qdec, kdec, gblk, *[job[0] for job in cast_jobs])
    return outs[:3], tuple(outs[3:])


def _scan_pre(sc_ref, params, scratch, tt):
    bmat_ref, lam_ref, cmat_ref, dskip_ref, convw_ref, convb_ref, wa_ref, ba_ref, wx_ref, bxb_ref, sp_ref = params
    tm, hbuf, hst, xpad, abuf, bbuf, lst, res = scratch
    rows = tt * SEQ_TILE
    pad_rows = (CONV_W - 1) * SEQ_TILE

    for s in range(SEQ_TILE):
        for j in range(D_SCAN // LANES):
            tm[j, pl.ds(s, tt, stride=SEQ_TILE), :] = sc_ref[s, :, j * LANES:(j + 1) * LANES]

    u = jnp.concatenate([tm[0], tm[1]], axis=-1)
    hbuf[...] = jnp.dot(u.astype(BF16), bmat_ref[...], preferred_element_type=F32)

    xpad[pad_rows:, :] = jnp.concatenate([tm[2], tm[3]], axis=-1)
    xc = convb_ref[...] + xpad[0:rows, :] * convw_ref[0:1, :]
    for j in range(1, CONV_W):
        xc = xc + xpad[j * SEQ_TILE:j * SEQ_TILE + rows, :] * convw_ref[j:j + 1, :]
    xcb = xc.astype(BF16)
    r = _sigmoid(jnp.dot(xcb, wa_ref[...], preferred_element_type=F32) + ba_ref[...])
    ig = _sigmoid(jnp.dot(xcb, wx_ref[...], preferred_element_type=F32) + bxb_ref[...])
    log_a = (-LRU_C) * r * sp_ref[...]
    a = jnp.exp(log_a)
    abuf[...] = a
    t1 = 1.0 - a * a
    bbuf[...] = jnp.where(t1 > 0.0, t1 * lax.rsqrt(t1), 0.0) * (ig * xc)

    return hst[:, 0:N_SSM], hst[:, N_SSM:2 * N_SSM], lst[...]


def _scan_steps(params, scratch, carry, t0, t1):
    lam_ref = params[1]
    tm, hbuf, hst, xpad, abuf, bbuf, lst, res = scratch
    lam_re = lam_ref[0:1, :]
    lam_im = lam_ref[1:2, :]
    hr, hi, hl = carry
    for t in range(t0, t1):
        lo = t * SEQ_TILE
        br = hbuf[lo:lo + SEQ_TILE, 0:N_SSM]
        bi = hbuf[lo:lo + SEQ_TILE, N_SSM:2 * N_SSM]
        hr, hi = lam_re * hr - lam_im * hi + br, lam_re * hi + lam_im * hr + bi
        hbuf[lo:lo + SEQ_TILE, 0:N_SSM] = hr
        hbuf[lo:lo + SEQ_TILE, N_SSM:2 * N_SSM] = hi
        hl = abuf[lo:lo + SEQ_TILE, :] * hl + bbuf[lo:lo + SEQ_TILE, :]
        bbuf[lo:lo + SEQ_TILE, :] = hl
    return hr, hi, hl


def _scan_post(params, scratch, carry, osl_out, tt):
    bmat_ref, lam_ref, cmat_ref, dskip_ref = params[:4]
    tm, hbuf, hst, xpad, abuf, bbuf, lst, res = scratch
    rows = tt * SEQ_TILE
    hr, hi, hl = carry
    hst[:, 0:N_SSM] = hr
    hst[:, N_SSM:2 * N_SSM] = hi
    lst[...] = hl
    for j in range(CONV_W - 1):
        xpad[j * SEQ_TILE:(j + 1) * SEQ_TILE, :] = xpad[rows + j * SEQ_TILE:rows + (j + 1) * SEQ_TILE, :]

    u = jnp.concatenate([tm[0], tm[1]], axis=-1)
    y = jnp.dot(hbuf[...].astype(BF16), cmat_ref[...], preferred_element_type=F32) + dskip_ref[...] * u
    o_lru = bbuf[...] * jnp.concatenate([tm[4], tm[5]], axis=-1)
    res[0] = y[:, :LANES]
    res[1] = y[:, LANES:]
    res[2] = o_lru[:, :LANES]
    res[3] = o_lru[:, LANES:]
    for s in range(SEQ_TILE):
        for j in range((D_SSM + D_LRU) // LANES):
            osl_out[s * tt:(s + 1) * tt, j * LANES:(j + 1) * LANES] = res[j, pl.ds(s, tt, stride=SEQ_TILE), :]


def _merge_head(x, oret, osl, weights):
    wglu_ref, ms_ref, wo_ref, nf_ref = weights[:4]
    ga = jnp.dot(_gelu_tanh(osl[:, :D_SSM]).astype(BF16), wglu_ref[...], preferred_element_type=F32)
    o_ssm = ga[:, :D_SSM] * _sigmoid(ga[:, D_SSM:])
    mixed = jnp.concatenate(
        [_rms_rows(oret, EPS), _rms_rows(o_ssm, EPS), _rms_rows(osl[:, D_SSM:], EPS)], axis=-1) * ms_ref[...]
    x = x + jnp.dot(mixed.astype(BF16), wo_ref[...], preferred_element_type=F32)
    return x, (_rms_rows(x, EPS) * nf_ref[...]).astype(BF16)


def _ffn_group(x, h, weights, lo, hi):
    wg_ref, wu_ref, wd_ref = weights[4:7]
    gate = jnp.dot(h, wg_ref[:, lo:hi], preferred_element_type=F32)
    up = jnp.dot(h, wu_ref[:, lo:hi], preferred_element_type=F32)
    ff = (gate * _sigmoid(gate) * up).astype(BF16)
    return x + jnp.dot(ff, wd_ref[lo:hi, :], preferred_element_type=F32)


def _mix_kernel(*refs, tt, nt, nb, final):
    sc_ref, x_ref, oret_ref, hs0_ref, hl0_ref, cc0_ref = refs[:6]
    scan_params = refs[6:17]
    merge_weights = refs[17:25]
    y_ref, hs_ref, hl_ref, cc_ref = refs[25:29]
    scratch = refs[29:37]
    osl_cur, osl_prev = refs[37:39]
    tm, hbuf, hst, xpad, abuf, bbuf, lst, res = scratch
    k = pl.program_id(0)
    c = lax.rem(k, nt)
    rows = tt * SEQ_TILE

    @pl.when(k == 0)
    def _():
        osl_prev[...] = jnp.zeros(osl_prev.shape, F32)

    @pl.when((c == 0) & (k < nb))
    def _():
        hst[...] = hs0_ref[...]
        lst[...] = hl0_ref[...]
        for j in range(CONV_W - 1):
            xpad[j * SEQ_TILE:(j + 1) * SEQ_TILE, :] = cc0_ref[j]

    carry = _scan_pre(sc_ref, scan_params, scratch, tt)
    carry = _scan_steps(scan_params, scratch, carry, 0, tt)
    _scan_post(scan_params, scratch, carry, osl_cur, tt)

    x, h = _merge_head(x_ref[...].reshape(rows, D_MODEL), oret_ref[...].reshape(rows, D_RET),
                       osl_prev[...], merge_weights)
    for lo, hi in FFN_SPLITS:
        x = _ffn_group(x, h, merge_weights, lo, hi)
    if final:
        x = _rms_rows(x, EPS) * merge_weights[7][...]
    y_ref[...] = x.reshape(SEQ_TILE, tt, D_MODEL)

    osl_prev[...] = osl_cur[...]

    @pl.when((c == nt - 1) & (k < nb))
    def _():
        hs_ref[...] = hst[...]
        hl_ref[...] = lst[...]
        for j in range(CONV_W - 1):
            cc_ref[j] = xpad[j * SEQ_TILE:(j + 1) * SEQ_TILE, :]


def _mix_call(sc, x, oret, hs0, hl0, cc0, sp, w, big, tt, layer, final):
    b, l, _ = sc.shape
    nt = l // tt
    nb = (b // SEQ_TILE) * nt
    rows = tt * SEQ_TILE
    scanned = lambda k: jnp.minimum(k, nb - 1)
    merged = lambda k: jnp.maximum(k - 1, 0)
    blk = lambda f, width: pl.BlockSpec((SEQ_TILE, tt, width), lambda k: (f(k) // nt, f(k) % nt, 0))
    seq_spec = lambda width: pl.BlockSpec((SEQ_TILE, width), lambda k: (scanned(k) // nt, 0))
    cc_spec = pl.BlockSpec((CONV_W - 1, SEQ_TILE, D_LRU), lambda k: (0, scanned(k) // nt, 0))
    seq0_spec = lambda width: pl.BlockSpec((None, SEQ_TILE, width), lambda k: (layer, scanned(k) // nt, 0))
    cc0_spec = pl.BlockSpec((None, CONV_W - 1, SEQ_TILE, D_LRU), lambda k: (layer, 0, scanned(k) // nt, 0))
    return pl.pallas_call(
        functools.partial(_mix_kernel, tt=tt, nt=nt, nb=nb, final=final),
        grid=(nb + 1,),
        in_specs=[
            blk(scanned, D_SCAN), blk(merged, D_MODEL), blk(merged, D_RET),
            seq0_spec(2 * N_SSM), seq0_spec(D_LRU), cc0_spec,
        ] + [_layer_spec(t.shape[1:], layer) for t in sp] + [
            _layer_spec((D_SSM, 2 * D_SSM), layer),
            _layer_spec((1, D_MODEL), layer), _const_spec((D_MODEL, D_MODEL)),
            _layer_spec((1, D_MODEL), layer),
            _const_spec((D_MODEL, D_FFN)), _const_spec((D_MODEL, D_FFN)), _const_spec((D_FFN, D_MODEL)),
            _const_spec((1, D_MODEL)),
        ],
        out_specs=[
            blk(merged, D_MODEL),
            seq_spec(2 * N_SSM), seq_spec(D_LRU), cc_spec,
        ],
        out_shape=[
            jax.ShapeDtypeStruct((b, l, D_MODEL), F32),
            jax.ShapeDtypeStruct((b, 2 * N_SSM), F32),
            jax.ShapeDtypeStruct((b, D_LRU), F32),
            jax.ShapeDtypeStruct((CONV_W - 1, b, D_LRU), F32),
        ],
        scratch_shapes=[
            pltpu.VMEM((D_SCAN // LANES, rows, LANES), F32),
            pltpu.VMEM((rows, 2 * N_SSM), F32),
            pltpu.VMEM((SEQ_TILE, 2 * N_SSM), F32),
            pltpu.VMEM((rows + (CONV_W - 1) * SEQ_TILE, D_LRU), F32),
            pltpu.VMEM((rows, D_LRU), F32),
            pltpu.VMEM((rows, D_LRU), F32),
            pltpu.VMEM((SEQ_TILE, D_LRU), F32),
            pltpu.VMEM(((D_SSM + D_LRU) // LANES, rows, LANES), F32),
            pltpu.VMEM((rows, D_SSM + D_LRU), F32),
            pltpu.VMEM((rows, D_SSM + D_LRU), F32),
        ],
        compiler_params=pltpu.CompilerParams(
            dimension_semantics=("arbitrary",), vmem_limit_bytes=VMEM_LIMIT),
        name="mix",
    )(sc, x, oret, hs0, hl0, cc0, *sp,
      w['w_glu'], w['mix_scale'], big[0], w['norm_ffn'], *big[1:], w['norm_final'])


def _block_diag(t):
    n, r, c = t.shape
    cols = jnp.swapaxes(t, 0, 1).reshape(r, n * c)
    on_diag = jnp.arange(n)[:, None] == (jnp.arange(n * c) // c)[None, :]
    return jnp.where(on_diag[:, None, :], cols[None], 0.0).reshape(n * r, n * c)


def _scan_params(a_re, a_im, b_re, b_im, c_re, c_im, d_skip, log_dt,
                 conv_w, conv_b, w_a, b_a, w_x, b_x, lam):
    dt = jnp.exp(log_dt)[:, None]
    mag = jnp.exp(a_re * dt)
    lb_re = mag * jnp.cos(a_im * dt)
    lb_im = mag * jnp.sin(a_im * dt)
    den = a_re * a_re + a_im * a_im
    f_re = ((lb_re - 1.0) * a_re + lb_im * a_im) / den
    f_im = (lb_im * a_re - (lb_re - 1.0) * a_im) / den
    bb_re = f_re[..., None] * b_re - f_im[..., None] * b_im
    bb_im = f_re[..., None] * b_im + f_im[..., None] * b_re
    bmat = jnp.concatenate([_block_diag(jnp.swapaxes(bb_re, 1, 2)), _block_diag(jnp.swapaxes(bb_im, 1, 2))],
                           axis=1).astype(BF16)
    cmat = jnp.concatenate([_block_diag(jnp.swapaxes(c_re, 1, 2)), _block_diag(jnp.swapaxes(-c_im, 1, 2))],
                           axis=0).astype(BF16)
    lam_bar = jnp.stack([lb_re.reshape(N_SSM), lb_im.reshape(N_SSM)])
    bd_lru = lambda t: _block_diag(t).astype(BF16)
    row = lambda t: t.reshape(1, -1)
    return (bmat, lam_bar, cmat, row(d_skip),
            conv_w, row(conv_b), bd_lru(w_a), row(b_a), bd_lru(w_x), row(b_x),
            row(jax.nn.softplus(-lam)))


def _rotary_table(pos_offset, l):
    pos = pos_offset + jnp.arange(l, dtype=jnp.int32)
    inv_freq = ROPE_BASE ** (-jnp.arange(0, HEAD_DIM, 2, dtype=F32) / HEAD_DIM)
    ang = pos.astype(F32)[:, None] * inv_freq[None, :]
    cos, sin = jnp.cos(ang), jnp.sin(ang)
    return jnp.concatenate([cos, cos, -sin, sin], axis=-1)


def _trunk(x, pos_offset, s_ret, s_ssm, s_lru, c_conv, w, sp, big_f32, big_bf16):
    b, l, _ = x.shape
    depth = s_ssm.shape[0]
    tb = min(RET_BLOCK, l)
    tt = min(SCAN_BLOCK, l)
    nseq = min(b, max(1, RET_ROWS // tb), RET_SEQS_MAX)
    assert l % tb == 0 and l % tt == 0 and b % SEQ_TILE == 0 and b % nseq == 0
    rot = _rotary_table(pos_offset, l)
    tables = _ret_tables(tb)
    cast_in_pr = big_bf16 is None and (b // nseq) * (l // tb) + 1 >= _cast_steps()
    if big_bf16 is None and not cast_in_pr:
        big_bf16 = [tuple(t[i].astype(BF16) for t in big_f32) for i in range(depth)]
    w_in_next = big_f32[0][0].astype(BF16) if cast_in_pr else None
    zero_ret = s_ret is None
    if zero_ret:
        s_ret = jnp.zeros((1, nseq, RET_HEADS, HEAD_DIM, HEAD_DIM), F32)
    ret_all = jnp.zeros((depth, b, RET_HEADS, HEAD_DIM, HEAD_DIM), F32)
    hs0 = jnp.concatenate([s_ssm[..., 0].reshape(depth, b, N_SSM), s_ssm[..., 1].reshape(depth, b, N_SSM)], axis=-1)
    cc0 = jnp.swapaxes(c_conv, 1, 2)
    xs = x
    big_out, new_hs, new_lru, new_cc = [], [], [], []
    for i in range(depth):
        w_in = w_in_next if cast_in_pr else big_bf16[i][0]
        (sc, oret, ret_all), cast = _pr_call(xs, w['norm_mix'], w_in, rot, s_ret, zero_ret, ret_all, tables,
                                             tb, nseq, i, _cast_jobs(big_f32, i) if cast_in_pr else ())
        if cast_in_pr:
            big_out.append((w_in,) + cast[:4])
            w_in_next = cast[4] if len(cast) > 4 else None
        else:
            big_out.append(big_bf16[i])
        xs, hs, hl, cc = _mix_call(sc, xs, oret, hs0, s_lru, cc0, sp, w, big_out[i][1:], tt, i,
                                   final=(i == depth - 1))
        new_hs.append(hs)
        new_lru.append(hl)
        new_cc.append(cc)
    hs = jnp.stack(new_hs)
    new_ssm = jnp.stack([hs[..., :N_SSM].reshape(depth, b, SSM_GROUPS, SSM_STATE),
                         hs[..., N_SSM:].reshape(depth, b, SSM_GROUPS, SSM_STATE)], axis=-1)
    return (xs, ret_all, new_ssm, jnp.stack(new_lru), jnp.swapaxes(jnp.stack(new_cc), 1, 2)), big_out


def _prepare_weights(norm_mix, mix_scale, ssm_w_glu, norm_ffn, norm_final):
    depth = norm_mix.shape[0]
    rows = lambda t: t.reshape(depth, 1, D_MODEL)
    return {
        'norm_mix': rows(norm_mix), 'mix_scale': rows(mix_scale), 'w_glu': ssm_w_glu.astype(BF16),
        'norm_ffn': rows(norm_ffn), 'norm_final': norm_final.reshape(1, D_MODEL),
    }


def kernel(x_prompt, x_sample, state_ret, state_ssm, state_lru, cache_conv, norm_mix, w_in, mix_scale, w_out, ssm_a_re, ssm_a_im, ssm_b_re, ssm_b_im, ssm_c_re, ssm_c_im, ssm_d, ssm_log_dt, ssm_w_glu, lru_conv_w, lru_conv_b, lru_w_a, lru_b_a, lru_w_x, lru_b_x, lru_lambda, norm_ffn, w_ffn_gate, w_ffn_up, w_ffn_down, norm_final):
    w = _prepare_weights(norm_mix, mix_scale, ssm_w_glu, norm_ffn, norm_final)
    depth = norm_mix.shape[0]
    sp = jax.vmap(_scan_params)(ssm_a_re, ssm_a_im, ssm_b_re, ssm_b_im, ssm_c_re, ssm_c_im, ssm_d, ssm_log_dt,
                                lru_conv_w, lru_conv_b, lru_w_a, lru_b_a, lru_w_x, lru_b_x, lru_lambda)
    big_f32 = (w_in, w_out, w_ffn_gate, w_ffn_up, w_ffn_down)
    bp = x_prompt.shape[0]
    z_ssm = jnp.zeros((depth, bp, SSM_GROUPS, SSM_STATE, 2), F32)
    z_lru = jnp.zeros((depth, bp, D_LRU), F32)
    z_conv = jnp.zeros((depth, bp, CONV_W - 1, D_LRU), F32)
    out_p, big_bf16 = _trunk(x_prompt, 0, None, z_ssm, z_lru, z_conv, w, sp, big_f32, None)
    out_s, _ = _trunk(x_sample, PAST_LEN, state_ret, state_ssm, state_lru, cache_conv, w, sp, big_f32, big_bf16)
    return (out_p[0], out_s[0]) + tuple(out_p[1:]) + tuple(out_s[1:])
```

```python
import functools
import math

import jax
import jax.numpy as jnp
from jax import lax
from jax.experimental import pallas as pl
from jax.experimental.pallas import tpu as pltpu

F32 = jnp.float32
BF16 = jnp.bfloat16

D_MODEL = 1024
D_RET = 512
D_SSM = 256
D_LRU = 256
RET_HEADS = 4
HEAD_DIM = 128
RET_CHUNK = 64
ROPE_BASE = 10000.0
SSM_GROUP = 16
SSM_GROUPS = 16
SSM_STATE = 64
N_SSM = SSM_GROUPS * SSM_STATE
LRU_BLOCKS = 4
CONV_W = 4
LRU_C = 8.0
D_FFN = 2816
D_IN = 4 * D_RET + D_SSM + 2 * D_LRU
D_SCAN = D_SSM + 2 * D_LRU
EPS = 1e-6
GN_EPS = 1e-5

LANES = 128
SEQ_TILE = 8
PROJ_SEQS = 2
CAST_ROWS_UP = 32
CAST_ROWS_DOWN = 128
RET_BLOCK = 256
RET_ROWS = 1024
RET_SEQS_MAX = 8
PAST_LEN = 1024
SCAN_BLOCK = 64
FFN_SPLITS = ((0, 1536), (1536, D_FFN))
VMEM_LIMIT = 60 * 1024 * 1024


def _const_spec(shape):
    nd = len(shape)
    return pl.BlockSpec(shape, lambda *_: (0,) * nd, pipeline_mode=pl.Buffered(1))


def _layer_spec(shape, layer):
    nd = len(shape)
    return pl.BlockSpec((None,) + tuple(shape), lambda *_: (layer,) + (0,) * nd, pipeline_mode=pl.Buffered(1))


def _sigmoid(x):
    return 0.5 * (1.0 + jnp.tanh(0.5 * x))


def _gelu_tanh(x):
    return 0.5 * x * (1.0 + jnp.tanh(math.sqrt(2.0 / math.pi) * (x + 0.044715 * (x * x * x))))


def _rms_rows(x, eps):
    return x * lax.rsqrt(jnp.mean(x * x, axis=-1, keepdims=True) + eps)


def _proj_block(x, nw_ref, w_ref, rot_ref, qkv_out, g_out, sc_ref, nseq, tb):
    cos2 = rot_ref[:, :HEAD_DIM]
    sin2 = rot_ref[:, HEAD_DIM:]
    q_scale = HEAD_DIM ** -0.5
    h = (_rms_rows(x, EPS) * nw_ref[...]).astype(BF16)
    for j0 in range(0, nseq, PROJ_SEQS):
        j1 = min(j0 + PROJ_SEQS, nseq)
        proj = jnp.dot(h[j0 * tb:j1 * tb], w_ref[...], preferred_element_type=F32)
        for j in range(j0, j1):
            pj = proj[(j - j0) * tb:(j - j0 + 1) * tb]
            for head in range(2 * RET_HEADS):
                lo = head * HEAD_DIM
                xh = pj[:, lo:lo + HEAD_DIM]
                r = xh * cos2 + pltpu.roll(xh, HEAD_DIM // 2, axis=1) * sin2
                if head < RET_HEADS:
                    r = r * q_scale
                qkv_out[j, :, lo:lo + HEAD_DIM] = r.astype(BF16)
            qkv_out[j, :, 2 * D_RET:3 * D_RET] = pj[:, 2 * D_RET:3 * D_RET].astype(BF16)
            gg = pj[:, 3 * D_RET:4 * D_RET]
            g_out[j] = gg * _sigmoid(gg)
            sc_ref[j, :, :D_SSM + D_LRU] = pj[:, 4 * D_RET:4 * D_RET + D_SSM + D_LRU]
            sc_ref[j, :, D_SSM + D_LRU:] = _gelu_tanh(pj[:, 4 * D_RET + D_SSM + D_LRU:])


def _ret_block(qkv, g, tables, s_scr, o_ref, nseq):
    dmask_ref, qdec_ref, kdec_ref, gblk_ref = tables
    for j in range(nseq):
        for h in range(RET_HEADS):
            lo = h * HEAD_DIM
            q = qkv[j, :, lo:lo + HEAD_DIM]
            k = qkv[j, :, D_RET + lo:D_RET + lo + HEAD_DIM]
            v = qkv[j, :, 2 * D_RET + lo:2 * D_RET + lo + HEAD_DIM]
            s = s_scr[j, h]
            scores = lax.dot_general(q, k, (((1,), (1,)), ((), ())), preferred_element_type=F32)
            a = (scores * dmask_ref[h]).astype(BF16)
            o = jnp.dot(a, v, preferred_element_type=F32)
            o = o + jnp.dot(q, s.astype(BF16), preferred_element_type=F32) * qdec_ref[h]
            kd = (k.astype(F32) * kdec_ref[h]).astype(BF16)
            u = lax.dot_general(kd, v, (((0,), (0,)), ((), ())), preferred_element_type=F32)
            s_scr[j, h] = gblk_ref[h] * s + u
            oc = o - jnp.mean(o, axis=-1, keepdims=True)
            on = oc * lax.rsqrt(jnp.mean(oc * oc, axis=-1, keepdims=True) + GN_EPS)
            o_ref[j, :, lo:lo + HEAD_DIM] = on * g[j, :, lo:lo + HEAD_DIM]


def _pr_kernel(*refs, nseq, tb, nt, nb, n_cast):
    x_ref, nw_ref, w_ref, rot_ref, s0_ref, s_all_ref, dmask_ref, qdec_ref, kdec_ref, gblk_ref = refs[:10]
    f32_refs = refs[10:10 + n_cast]
    sc_ref, o_ref, sfin_ref = refs[10 + n_cast:13 + n_cast]
    bf16_refs = refs[13 + n_cast:13 + 2 * n_cast]
    s_scr, qkv_cur, qkv_prev, g_cur, g_prev = refs[13 + 2 * n_cast:]
    for src, dst in zip(f32_refs, bf16_refs):
        dst[...] = src[...].astype(BF16)
    del s_all_ref
    k = pl.program_id(0)
    c_ret = lax.rem(jnp.maximum(k - 1, 0), nt)

    @pl.when(k == 0)
    def _():
        qkv_prev[...] = jnp.zeros(qkv_prev.shape, BF16)
        g_prev[...] = jnp.zeros(g_prev.shape, F32)

    @pl.when(c_ret == 0)
    def _():
        s_scr[...] = s0_ref[...]

    _proj_block(x_ref[...].reshape(nseq * tb, D_MODEL), nw_ref, w_ref, rot_ref, qkv_cur, g_cur, sc_ref, nseq, tb)
    _ret_block(qkv_prev, g_prev, (dmask_ref, qdec_ref, kdec_ref, gblk_ref), s_scr, o_ref, nseq)
    qkv_prev[...] = qkv_cur[...]
    g_prev[...] = g_cur[...]

    @pl.when((c_ret == nt - 1) & (k >= 1))
    def _():
        sfin_ref[...] = s_scr[...]


def _ret_tables(tb):
    h = jnp.arange(RET_HEADS, dtype=F32)
    gamma = 1.0 - 2.0 ** (-5.0 - h)
    log_g = jnp.log(gamma)
    idx = jnp.arange(tb, dtype=F32)
    chunk = jnp.arange(tb, dtype=jnp.int32) // RET_CHUNK
    diff = idx[:, None] - idx[None, :]
    same = chunk[:, None] == chunk[None, :]
    earlier = chunk[:, None] > chunk[None, :]
    expo = jnp.where(same, jnp.abs(diff), diff)
    dmask = jnp.where((same | earlier)[None], jnp.exp(log_g[:, None, None] * expo[None]), 0.0)
    qdec = jnp.exp(log_g[:, None] * (idx + 1.0)[None, :])
    kdec = jnp.exp(log_g[:, None] * (tb - 1.0 - idx)[None, :])
    gblk = jnp.exp(log_g * tb)
    wide = lambda t: jnp.broadcast_to(t[:, :, None], t.shape + (HEAD_DIM,))
    return dmask, wide(qdec), wide(kdec), jnp.broadcast_to(gblk[:, None, None], (RET_HEADS, 1, HEAD_DIM))


def _cast_steps():
    return max(D_MODEL // CAST_ROWS_UP, D_FFN // CAST_ROWS_DOWN)


def _cast_jobs(big_f32, layer):
    w_in, w_out, w_gate, w_up, w_down = big_f32
    jobs = [(w_out, layer, CAST_ROWS_UP), (w_gate, layer, CAST_ROWS_UP), (w_up, layer, CAST_ROWS_UP),
            (w_down, layer, CAST_ROWS_DOWN)]
    if layer + 1 < w_in.shape[0]:
        jobs.append((w_in, layer + 1, CAST_ROWS_UP))
    return jobs


def _pr_call(x, nw, w_in, rot, s0, s0_shared, s_all, tables, tb, nseq, layer, cast_jobs=()):
    b, l, _ = x.shape
    nt = l // tb
    nb = (b // nseq) * nt
    cast_in, cast_out, cast_shapes = [], [], []
    for arr, lyr, rows in cast_jobs:
        shape = arr.shape[1:]
        assert shape[0] % rows == 0 and nb + 1 >= shape[0] // rows
        last = shape[0] // rows - 1
        cast_in.append(pl.BlockSpec((None, rows, shape[1]),
                                    lambda k, last=last, lyr=lyr: (lyr, jnp.minimum(k, last), 0)))
        cast_out.append(pl.BlockSpec((rows, shape[1]), lambda k, last=last: (jnp.minimum(k, last), 0)))
        cast_shapes.append(jax.ShapeDtypeStruct(shape, BF16))
    dmask, qdec, kdec, gblk = tables
    projected = lambda k: jnp.minimum(k, nb - 1)
    retained = lambda k: jnp.maximum(k - 1, 0)
    blk = lambda f, width: pl.BlockSpec((nseq, tb, width), lambda k: (f(k) // nt, f(k) % nt, 0))
    state_block = (None, nseq, RET_HEADS, HEAD_DIM, HEAD_DIM)
    if s0_shared:
        s0_spec = pl.BlockSpec(state_block, lambda k: (0, 0, 0, 0, 0))
    else:
        s0_spec = pl.BlockSpec(state_block, lambda k: (layer, retained(k) // nt, 0, 0, 0))
    outs = pl.pallas_call(
        functools.partial(_pr_kernel, nseq=nseq, tb=tb, nt=nt, nb=nb, n_cast=len(cast_jobs)),
        grid=(nb + 1,),
        in_specs=[
            blk(projected, D_MODEL),
            _layer_spec((1, D_MODEL), layer),
            _const_spec((D_MODEL, D_IN)),
            pl.BlockSpec((tb, 2 * HEAD_DIM), lambda k: (projected(k) % nt, 0)),
            s0_spec,
            pl.BlockSpec(memory_space=pl.ANY),
            _const_spec((RET_HEADS, tb, tb)),
            _const_spec((RET_HEADS, tb, HEAD_DIM)),
            _const_spec((RET_HEADS, tb, HEAD_DIM)),
            _const_spec((RET_HEADS, 1, HEAD_DIM)),
        ] + cast_in,
        out_specs=[
            blk(projected, D_SCAN),
            blk(retained, D_RET),
            pl.BlockSpec(state_block, lambda k: (layer, retained(k) // nt, 0, 0, 0)),
        ] + cast_out,
        out_shape=[
            jax.ShapeDtypeStruct((b, l, D_SCAN), F32),
            jax.ShapeDtypeStruct((b, l, D_RET), F32),
            jax.ShapeDtypeStruct(s_all.shape, F32),
        ] + cast_shapes,
        input_output_aliases={5: 2},
        scratch_shapes=[
            pltpu.VMEM((nseq, RET_HEADS, HEAD_DIM, HEAD_DIM), F32),
            pltpu.VMEM((nseq, tb, 3 * D_RET), BF16),
            pltpu.VMEM((nseq, tb, 3 * D_RET), BF16),
            pltpu.VMEM((nseq, tb, D_RET), F32),
            pltpu.VMEM((nseq, tb, D_RET), F32),
        ],
        compiler_params=pltpu.CompilerParams(
            dimension_semantics=("arbitrary",), vmem_limit_bytes=VMEM_LIMIT),
        name="pr",
    )(x, nw, w_in, rot, s0, s_all, dmask, qdec, kdec, gblk, *[job[0] for job in cast_jobs])
    return outs[:3], tuple(outs[3:])


def _scan_pre(sc_ref, params, scratch, tt):
    bmat_ref, lam_ref, cmat_ref, dskip_ref, convw_ref, convb_ref, wa_ref, ba_ref, wx_ref, bxb_ref, sp_ref = params
    tm, hbuf, hst, xpad, abuf, bbuf, lst, res = scratch
    rows = tt * SEQ_TILE
    pad_rows = (CONV_W - 1) * SEQ_TILE

    for s in range(SEQ_TILE):
        for j in range(D_SCAN // LANES):
            tm[j, pl.ds(s, tt, stride=SEQ_TILE), :] = sc_ref[s, :, j * LANES:(j + 1) * LANES]

    u = jnp.concatenate([tm[0], tm[1]], axis=-1)
    hbuf[...] = jnp.dot(u.astype(BF16), bmat_ref[...], preferred_element_type=F32)

    xpad[pad_rows:, :] = jnp.concatenate([tm[2], tm[3]], axis=-1)
    xc = convb_ref[...] + xpad[0:rows, :] * convw_ref[0:1, :]
    for j in range(1, CONV_W):
        xc = xc + xpad[j * SEQ_TILE:j * SEQ_TILE + rows, :] * convw_ref[j:j + 1, :]
    xcb = xc.astype(BF16)
    r = _sigmoid(jnp.dot(xcb, wa_ref[...], preferred_element_type=F32) + ba_ref[...])
    ig = _sigmoid(jnp.dot(xcb, wx_ref[...], preferred_element_type=F32) + bxb_ref[...])
    log_a = (-LRU_C) * r * sp_ref[...]
    a = jnp.exp(log_a)
    abuf[...] = a
    t1 = 1.0 - a * a
    bbuf[...] = jnp.where(t1 > 0.0, t1 * lax.rsqrt(t1), 0.0) * (ig * xc)

    return hst[:, 0:N_SSM], hst[:, N_SSM:2 * N_SSM], lst[...]


def _scan_steps(params, scratch, carry, t0, t1):
    lam_ref = params[1]
    tm, hbuf, hst, xpad, abuf, bbuf, lst, res = scratch
    lam_re = lam_ref[0:1, :]
    lam_im = lam_ref[1:2, :]
    hr, hi, hl = carry
    for t in range(t0, t1):
        lo = t * SEQ_TILE
        br = hbuf[lo:lo + SEQ_TILE, 0:N_SSM]
        bi = hbuf[lo:lo + SEQ_TILE, N_SSM:2 * N_SSM]
        hr, hi = lam_re * hr - lam_im * hi + br, lam_re * hi + lam_im * hr + bi
        hbuf[lo:lo + SEQ_TILE, 0:N_SSM] = hr
        hbuf[lo:lo + SEQ_TILE, N_SSM:2 * N_SSM] = hi
        hl = abuf[lo:lo + SEQ_TILE, :] * hl + bbuf[lo:lo + SEQ_TILE, :]
        bbuf[lo:lo + SEQ_TILE, :] = hl
    return hr, hi, hl


def _scan_post(params, scratch, carry, osl_out, tt):
    bmat_ref, lam_ref, cmat_ref, dskip_ref = params[:4]
    tm, hbuf, hst, xpad, abuf, bbuf, lst, res = scratch
    rows = tt * SEQ_TILE
    hr, hi, hl = carry
    hst[:, 0:N_SSM] = hr
    hst[:, N_SSM:2 * N_SSM] = hi
    lst[...] = hl
    for j in range(CONV_W - 1):
        xpad[j * SEQ_TILE:(j + 1) * SEQ_TILE, :] = xpad[rows + j * SEQ_TILE:rows + (j + 1) * SEQ_TILE, :]

    u = jnp.concatenate([tm[0], tm[1]], axis=-1)
    y = jnp.dot(hbuf[...].astype(BF16), cmat_ref[...], preferred_element_type=F32) + dskip_ref[...] * u
    o_lru = bbuf[...] * jnp.concatenate([tm[4], tm[5]], axis=-1)
    res[0] = y[:, :LANES]
    res[1] = y[:, LANES:]
    res[2] = o_lru[:, :LANES]
    res[3] = o_lru[:, LANES:]
    for s in range(SEQ_TILE):
        for j in range((D_SSM + D_LRU) // LANES):
            osl_out[s * tt:(s + 1) * tt, j * LANES:(j + 1) * LANES] = res[j, pl.ds(s, tt, stride=SEQ_TILE), :]


def _merge_mixed(oret, osl, weights):
    wglu_ref, ms_ref = weights[:2]
    ga = jnp.dot(_gelu_tanh(osl[:, :D_SSM]).astype(BF16), wglu_ref[...], preferred_element_type=F32)
    o_ssm = ga[:, :D_SSM] * _sigmoid(ga[:, D_SSM:])
    mixed = jnp.concatenate(
        [_rms_rows(oret, EPS), _rms_rows(o_ssm, EPS), _rms_rows(osl[:, D_SSM:], EPS)], axis=-1) * ms_ref[...]
    return mixed.astype(BF16)


def _merge_project(x, mixed, weights):
    wo_ref, nf_ref = weights[2:4]
    x = x + jnp.dot(mixed, wo_ref[...], preferred_element_type=F32)
    return x, (_rms_rows(x, EPS) * nf_ref[...]).astype(BF16)


def _ffn_group(x, h, weights, lo, hi):
    wg_ref, wu_ref, wd_ref = weights[4:7]
    gate = jnp.dot(h, wg_ref[:, lo:hi], preferred_element_type=F32)
    up = jnp.dot(h, wu_ref[:, lo:hi], preferred_element_type=F32)
    ff = (gate * _sigmoid(gate) * up).astype(BF16)
    return x + jnp.dot(ff, wd_ref[lo:hi, :], preferred_element_type=F32)


def _mix_kernel(*refs, tt, nt, nb, final):
    sc_ref, x_ref, oret_ref, hs0_ref, hl0_ref, cc0_ref = refs[:6]
    scan_params = refs[6:17]
    merge_weights = refs[17:25]
    y_ref, hs_ref, hl_ref, cc_ref = refs[25:29]
    scratch = refs[29:37]
    osl, mixed = refs[37:39]
    tm, hbuf, hst, xpad, abuf, bbuf, lst, res = scratch
    k = pl.program_id(0)
    c = lax.rem(k, nt)
    rows = tt * SEQ_TILE

    @pl.when(k == 0)
    def _():
        mixed[...] = jnp.zeros(mixed.shape, BF16)

    @pl.when((c == 0) & (k < nb))
    def _():
        hst[...] = hs0_ref[...]
        lst[...] = hl0_ref[...]
        for j in range(CONV_W - 1):
            xpad[j * SEQ_TILE:(j + 1) * SEQ_TILE, :] = cc0_ref[j]

    carry = _scan_pre(sc_ref, scan_params, scratch, tt)
    carry = _scan_steps(scan_params, scratch, carry, 0, tt)
    _scan_post(scan_params, scratch, carry, osl, tt)

    x, h = _merge_project(x_ref[...].reshape(rows, D_MODEL), mixed[...], merge_weights)
    for lo, hi in FFN_SPLITS:
        x = _ffn_group(x, h, merge_weights, lo, hi)
    if final:
        x = _rms_rows(x, EPS) * merge_weights[7][...]
    y_ref[...] = x.reshape(SEQ_TILE, tt, D_MODEL)

    mixed[...] = _merge_mixed(oret_ref[...].reshape(rows, D_RET), osl[...], merge_weights)

    @pl.when((c == nt - 1) & (k < nb))
    def _():
        hs_ref[...] = hst[...]
        hl_ref[...] = lst[...]
        for j in range(CONV_W - 1):
            cc_ref[j] = xpad[j * SEQ_TILE:(j + 1) * SEQ_TILE, :]


def _mix_call(sc, x, oret, hs0, hl0, cc0, sp, w, big, tt, layer, final):
    b, l, _ = sc.shape
    nt = l // tt
    nb = (b // SEQ_TILE) * nt
    rows = tt * SEQ_TILE
    scanned = lambda k: jnp.minimum(k, nb - 1)
    merged = lambda k: jnp.maximum(k - 1, 0)
    blk = lambda f, width: pl.BlockSpec((SEQ_TILE, tt, width), lambda k: (f(k) // nt, f(k) % nt, 0))
    seq_spec = lambda width: pl.BlockSpec((SEQ_TILE, width), lambda k: (scanned(k) // nt, 0))
    cc_spec = pl.BlockSpec((CONV_W - 1, SEQ_TILE, D_LRU), lambda k: (0, scanned(k) // nt, 0))
    seq0_spec = lambda width: pl.BlockSpec((None, SEQ_TILE, width), lambda k: (layer, scanned(k) // nt, 0))
    cc0_spec = pl.BlockSpec((None, CONV_W - 1, SEQ_TILE, D_LRU), lambda k: (layer, 0, scanned(k) // nt, 0))
    return pl.pallas_call(
        functools.partial(_mix_kernel, tt=tt, nt=nt, nb=nb, final=final),
        grid=(nb + 1,),
        in_specs=[
            blk(scanned, D_SCAN), blk(merged, D_MODEL), blk(scanned, D_RET),
            seq0_spec(2 * N_SSM), seq0_spec(D_LRU), cc0_spec,
        ] + [_layer_spec(t.shape[1:], layer) for t in sp] + [
            _layer_spec((D_SSM, 2 * D_SSM), layer),
            _layer_spec((1, D_MODEL), layer), _const_spec((D_MODEL, D_MODEL)),
            _layer_spec((1, D_MODEL), layer),
            _const_spec((D_MODEL, D_FFN)), _const_spec((D_MODEL, D_FFN)), _const_spec((D_FFN, D_MODEL)),
            _const_spec((1, D_MODEL)),
        ],
        out_specs=[
            blk(merged, D_MODEL),
            seq_spec(2 * N_SSM), seq_spec(D_LRU), cc_spec,
        ],
        out_shape=[
            jax.ShapeDtypeStruct((b, l, D_MODEL), F32),
            jax.ShapeDtypeStruct((b, 2 * N_SSM), F32),
            jax.ShapeDtypeStruct((b, D_LRU), F32),
            jax.ShapeDtypeStruct((CONV_W - 1, b, D_LRU), F32),
        ],
        scratch_shapes=[
            pltpu.VMEM((D_SCAN // LANES, rows, LANES), F32),
            pltpu.VMEM((rows, 2 * N_SSM), F32),
            pltpu.VMEM((SEQ_TILE, 2 * N_SSM), F32),
            pltpu.VMEM((rows + (CONV_W - 1) * SEQ_TILE, D_LRU), F32),
            pltpu.VMEM((rows, D_LRU), F32),
            pltpu.VMEM((rows, D_LRU), F32),
            pltpu.VMEM((SEQ_TILE, D_LRU), F32),
            pltpu.VMEM(((D_SSM + D_LRU) // LANES, rows, LANES), F32),
            pltpu.VMEM((rows, D_SSM + D_LRU), F32),
            pltpu.VMEM((rows, D_MODEL), BF16),
        ],
        compiler_params=pltpu.CompilerParams(
            dimension_semantics=("arbitrary",), vmem_limit_bytes=VMEM_LIMIT),
        name="mix",
    )(sc, x, oret, hs0, hl0, cc0, *sp,
      w['w_glu'], w['mix_scale'], big[0], w['norm_ffn'], *big[1:], w['norm_final'])


def _block_diag(t):
    n, r, c = t.shape
    cols = jnp.swapaxes(t, 0, 1).reshape(r, n * c)
    on_diag = jnp.arange(n)[:, None] == (jnp.arange(n * c) // c)[None, :]
    return jnp.where(on_diag[:, None, :], cols[None], 0.0).reshape(n * r, n * c)


def _scan_params(a_re, a_im, b_re, b_im, c_re, c_im, d_skip, log_dt,
                 conv_w, conv_b, w_a, b_a, w_x, b_x, lam):
    dt = jnp.exp(log_dt)[:, None]
    mag = jnp.exp(a_re * dt)
    lb_re = mag * jnp.cos(a_im * dt)
    lb_im = mag * jnp.sin(a_im * dt)
    den = a_re * a_re + a_im * a_im
    f_re = ((lb_re - 1.0) * a_re + lb_im * a_im) / den
    f_im = (lb_im * a_re - (lb_re - 1.0) * a_im) / den
    bb_re = f_re[..., None] * b_re - f_im[..., None] * b_im
    bb_im = f_re[..., None] * b_im + f_im[..., None] * b_re
    bmat = jnp.concatenate([_block_diag(jnp.swapaxes(bb_re, 1, 2)), _block_diag(jnp.swapaxes(bb_im, 1, 2))],
                           axis=1).astype(BF16)
    cmat = jnp.concatenate([_block_diag(jnp.swapaxes(c_re, 1, 2)), _block_diag(jnp.swapaxes(-c_im, 1, 2))],
                           axis=0).astype(BF16)
    lam_bar = jnp.stack([lb_re.reshape(N_SSM), lb_im.reshape(N_SSM)])
    bd_lru = lambda t: _block_diag(t).astype(BF16)
    row = lambda t: t.reshape(1, -1)
    return (bmat, lam_bar, cmat, row(d_skip),
            conv_w, row(conv_b), bd_lru(w_a), row(b_a), bd_lru(w_x), row(b_x),
            row(jax.nn.softplus(-lam)))


def _rotary_table(pos_offset, l):
    pos = pos_offset + jnp.arange(l, dtype=jnp.int32)
    inv_freq = ROPE_BASE ** (-jnp.arange(0, HEAD_DIM, 2, dtype=F32) / HEAD_DIM)
    ang = pos.astype(F32)[:, None] * inv_freq[None, :]
    cos, sin = jnp.cos(ang), jnp.sin(ang)
    return jnp.concatenate([cos, cos, -sin, sin], axis=-1)


def _trunk(x, pos_offset, s_ret, s_ssm, s_lru, c_conv, w, sp, big_f32, big_bf16):
    b, l, _ = x.shape
    depth = s_ssm.shape[0]
    tb = min(RET_BLOCK, l)
    tt = min(SCAN_BLOCK, l)
    nseq = min(b, max(1, RET_ROWS // tb), RET_SEQS_MAX)
    assert l % tb == 0 and l % tt == 0 and b % SEQ_TILE == 0 and b % nseq == 0
    rot = _rotary_table(pos_offset, l)
    tables = _ret_tables(tb)
    cast_in_pr = big_bf16 is None and (b // nseq) * (l // tb) + 1 >= _cast_steps()
    if big_bf16 is None and not cast_in_pr:
        big_bf16 = [tuple(t[i].astype(BF16) for t in big_f32) for i in range(depth)]
    w_in_next = big_f32[0][0].astype(BF16) if cast_in_pr else None
    zero_ret = s_ret is None
    if zero_ret:
        s_ret = jnp.zeros((1, nseq, RET_HEADS, HEAD_DIM, HEAD_DIM), F32)
    ret_all = jnp.zeros((depth, b, RET_HEADS, HEAD_DIM, HEAD_DIM), F32)
    hs0 = jnp.concatenate([s_ssm[..., 0].reshape(depth, b, N_SSM), s_ssm[..., 1].reshape(depth, b, N_SSM)], axis=-1)
    cc0 = jnp.swapaxes(c_conv, 1, 2)
    xs = x
    big_out, new_hs, new_lru, new_cc = [], [], [], []
    for i in range(depth):
        w_in = w_in_next if cast_in_pr else big_bf16[i][0]
        (sc, oret, ret_all), cast = _pr_call(xs, w['norm_mix'], w_in, rot, s_ret, zero_ret, ret_all, tables,
                                             tb, nseq, i, _cast_jobs(big_f32, i) if cast_in_pr else ())
        if cast_in_pr:
            big_out.append((w_in,) + cast[:4])
            w_in_next = cast[4] if len(cast) > 4 else None
        else:
            big_out.append(big_bf16[i])
        xs, hs, hl, cc = _mix_call(sc, xs, oret, hs0, s_lru, cc0, sp, w, big_out[i][1:], tt, i,
                                   final=(i == depth - 1))
        new_hs.append(hs)
        new_lru.append(hl)
        new_cc.append(cc)
    hs = jnp.stack(new_hs)
    new_ssm = jnp.stack([hs[..., :N_SSM].reshape(depth, b, SSM_GROUPS, SSM_STATE),
                         hs[..., N_SSM:].reshape(depth, b, SSM_GROUPS, SSM_STATE)], axis=-1)
    return (xs, ret_all, new_ssm, jnp.stack(new_lru), jnp.swapaxes(jnp.stack(new_cc), 1, 2)), big_out


def _prepare_weights(norm_mix, mix_scale, ssm_w_glu, norm_ffn, norm_final):
    depth = norm_mix.shape[0]
    rows = lambda t: t.reshape(depth, 1, D_MODEL)
    return {
        'norm_mix': rows(norm_mix), 'mix_scale': rows(mix_scale), 'w_glu': ssm_w_glu.astype(BF16),
        'norm_ffn': rows(norm_ffn), 'norm_final': norm_final.reshape(1, D_MODEL),
    }


def kernel(x_prompt, x_sample, state_ret, state_ssm, state_lru, cache_conv, norm_mix, w_in, mix_scale, w_out, ssm_a_re, ssm_a_im, ssm_b_re, ssm_b_im, ssm_c_re, ssm_c_im, ssm_d, ssm_log_dt, ssm_w_glu, lru_conv_w, lru_conv_b, lru_w_a, lru_b_a, lru_w_x, lru_b_x, lru_lambda, norm_ffn, w_ffn_gate, w_ffn_up, w_ffn_down, norm_final):
    w = _prepare_weights(norm_mix, mix_scale, ssm_w_glu, norm_ffn, norm_final)
    depth = norm_mix.shape[0]
    sp = jax.vmap(_scan_params)(ssm_a_re, ssm_a_im, ssm_b_re, ssm_b_im, ssm_c_re, ssm_c_im, ssm_d, ssm_log_dt,
                                lru_conv_w, lru_conv_b, lru_w_a, lru_b_a, lru_w_x, lru_b_x, lru_lambda)
    big_f32 = (w_in, w_out, w_ffn_gate, w_ffn_up, w_ffn_down)
    bp = x_prompt.shape[0]
    z_ssm = jnp.zeros((depth, bp, SSM_GROUPS, SSM_STATE, 2), F32)
    z_lru = jnp.zeros((depth, bp, D_LRU), F32)
    z_conv = jnp.zeros((depth, bp, CONV_W - 1, D_LRU), F32)
    out_p, big_bf16 = _trunk(x_prompt, 0, None, z_ssm, z_lru, z_conv, w, sp, big_f32, None)
    out_s, _ = _trunk(x_sample, PAST_LEN, state_ret, state_ssm, state_lru, cache_conv, w, sp, big_f32, big_bf16)
    return (out_p[0], out_s[0]) + tuple(out_p[1:]) + tuple(out_s[1:])
```

```python
import functools
import math

import jax
import jax.numpy as jnp
from jax import lax
from jax.experimental import pallas as pl
from jax.experimental.pallas import tpu as pltpu

F32 = jnp.float32
BF16 = jnp.bfloat16

D_MODEL = 1024
D_RET = 512
D_SSM = 256
D_LRU = 256
RET_HEADS = 4
HEAD_DIM = 128
RET_CHUNK = 64
ROPE_BASE = 10000.0
SSM_GROUP = 16
SSM_GROUPS = 16
SSM_STATE = 64
N_SSM = SSM_GROUPS * SSM_STATE
LRU_BLOCKS = 4
CONV_W = 4
LRU_C = 8.0
D_FFN = 2816
D_IN = 4 * D_RET + D_SSM + 2 * D_LRU
D_SCAN = D_SSM + 2 * D_LRU
EPS = 1e-6
GN_EPS = 1e-5

LANES = 128
SEQ_TILE = 8
PROJ_SEQS = 2
CAST_ROWS_UP = 32
CAST_ROWS_DOWN = 128
RET_BLOCK = 256
RET_ROWS = 1024
RET_SEQS_MAX = 8
PAST_LEN = 1024
SCAN_BLOCK = 64
FFN_SPLITS = ((0, 1536), (1536, D_FFN))
VMEM_LIMIT = 60 * 1024 * 1024


def _const_spec(shape):
    nd = len(shape)
    return pl.BlockSpec(shape, lambda *_: (0,) * nd, pipeline_mode=pl.Buffered(1))


def _layer_spec(shape, layer):
    nd = len(shape)
    return pl.BlockSpec((None,) + tuple(shape), lambda *_: (layer,) + (0,) * nd, pipeline_mode=pl.Buffered(1))


def _sigmoid(x):
    return 0.5 * (1.0 + jnp.tanh(0.5 * x))


def _gelu_tanh(x):
    return 0.5 * x * (1.0 + jnp.tanh(math.sqrt(2.0 / math.pi) * (x + 0.044715 * (x * x * x))))


def _rms_rows(x, eps):
    return x * lax.rsqrt(jnp.mean(x * x, axis=-1, keepdims=True) + eps)


def _proj_block(x, nw_ref, w_ref, rot_ref, qkv_out, g_out, sc_ref, nseq, tb):
    cos2 = rot_ref[:, :HEAD_DIM]
    sin2 = rot_ref[:, HEAD_DIM:]
    q_scale = HEAD_DIM ** -0.5
    h = (_rms_rows(x, EPS) * nw_ref[...]).astype(BF16)
    for j0 in range(0, nseq, PROJ_SEQS):
        j1 = min(j0 + PROJ_SEQS, nseq)
        proj = jnp.dot(h[j0 * tb:j1 * tb], w_ref[...], preferred_element_type=F32)
        for j in range(j0, j1):
            pj = proj[(j - j0) * tb:(j - j0 + 1) * tb]
            for head in range(2 * RET_HEADS):
                lo = head * HEAD_DIM
                xh = pj[:, lo:lo + HEAD_DIM]
                r = xh * cos2 + pltpu.roll(xh, HEAD_DIM // 2, axis=1) * sin2
                if head < RET_HEADS:
                    r = r * q_scale
                qkv_out[j, :, lo:lo + HEAD_DIM] = r.astype(BF16)
            qkv_out[j, :, 2 * D_RET:3 * D_RET] = pj[:, 2 * D_RET:3 * D_RET].astype(BF16)
            gg = pj[:, 3 * D_RET:4 * D_RET]
            g_out[j] = gg * _sigmoid(gg)
            sc_ref[j, :, :D_SSM + D_LRU] = pj[:, 4 * D_RET:4 * D_RET + D_SSM + D_LRU]
            sc_ref[j, :, D_SSM + D_LRU:] = _gelu_tanh(pj[:, 4 * D_RET + D_SSM + D_LRU:])


def _ret_block(qkv, g, tables, s_scr, o_ref, nseq):
    dmask_ref, qdec_ref, kdec_ref, gblk_ref = tables
    for j in range(nseq):
        for h in range(RET_HEADS):
            lo = h * HEAD_DIM
            q = qkv[j, :, lo:lo + HEAD_DIM]
            k = qkv[j, :, D_RET + lo:D_RET + lo + HEAD_DIM]
            v = qkv[j, :, 2 * D_RET + lo:2 * D_RET + lo + HEAD_DIM]
            s = s_scr[j, h]
            scores = lax.dot_general(q, k, (((1,), (1,)), ((), ())), preferred_element_type=F32)
            a = (scores * dmask_ref[h]).astype(BF16)
            o = jnp.dot(a, v, preferred_element_type=F32)
            o = o + jnp.dot(q, s.astype(BF16), preferred_element_type=F32) * qdec_ref[h]
            kd = (k.astype(F32) * kdec_ref[h]).astype(BF16)
            u = lax.dot_general(kd, v, (((0,), (0,)), ((), ())), preferred_element_type=F32)
            s_scr[j, h] = gblk_ref[h] * s + u
            oc = o - jnp.mean(o, axis=-1, keepdims=True)
            on = oc * lax.rsqrt(jnp.mean(oc * oc, axis=-1, keepdims=True) + GN_EPS)
            o_ref[j, :, lo:lo + HEAD_DIM] = on * g[j, :, lo:lo + HEAD_DIM]


def _pr_kernel(*refs, nseq, tb, nt, nb, n_cast):
    x_ref, nw_ref, w_ref, rot_ref, s0_ref, s_all_ref, dmask_ref, qdec_ref, kdec_ref, gblk_ref = refs[:10]
    f32_refs = refs[10:10 + n_cast]
    sc_ref, o_ref, sfin_ref = refs[10 + n_cast:13 + n_cast]
    bf16_refs = refs[13 + n_cast:13 + 2 * n_cast]
    s_scr, qkv_cur, qkv_prev, g_cur, g_prev = refs[13 + 2 * n_cast:]
    for src, dst in zip(f32_refs, bf16_refs):
        dst[...] = src[...].astype(BF16)
    del s_all_ref
    k = pl.program_id(0)
    c_ret = lax.rem(jnp.maximum(k - 1, 0), nt)

    @pl.when(k == 0)
    def _():
        qkv_prev[...] = jnp.zeros(qkv_prev.shape, BF16)
        g_prev[...] = jnp.zeros(g_prev.shape, F32)

    @pl.when(c_ret == 0)
    def _():
        s_scr[...] = s0_ref[...]

    _proj_block(x_ref[...].reshape(nseq * tb, D_MODEL), nw_ref, w_ref, rot_ref, qkv_cur, g_cur, sc_ref, nseq, tb)
    _ret_block(qkv_prev, g_prev, (dmask_ref, qdec_ref, kdec_ref, gblk_ref), s_scr, o_ref, nseq)
    qkv_prev[...] = qkv_cur[...]
    g_prev[...] = g_cur[...]

    @pl.when((c_ret == nt - 1) & (k >= 1))
    def _():
        sfin_ref[...] = s_scr[...]


def _ret_tables(tb):
    h = jnp.arange(RET_HEADS, dtype=F32)
    gamma = 1.0 - 2.0 ** (-5.0 - h)
    log_g = jnp.log(gamma)
    idx = jnp.arange(tb, dtype=F32)
    chunk = jnp.arange(tb, dtype=jnp.int32) // RET_CHUNK
    diff = idx[:, None] - idx[None, :]
    same = chunk[:, None] == chunk[None, :]
    earlier = chunk[:, None] > chunk[None, :]
    expo = jnp.where(same, jnp.abs(diff), diff)
    dmask = jnp.where((same | earlier)[None], jnp.exp(log_g[:, None, None] * expo[None]), 0.0)
    qdec = jnp.exp(log_g[:, None] * (idx + 1.0)[None, :])
    kdec = jnp.exp(log_g[:, None] * (tb - 1.0 - idx)[None, :])
    gblk = jnp.exp(log_g * tb)
    wide = lambda t: jnp.broadcast_to(t[:, :, None], t.shape + (HEAD_DIM,))
    return dmask, wide(qdec), wide(kdec), jnp.broadcast_to(gblk[:, None, None], (RET_HEADS, 1, HEAD_DIM))


def _cast_steps():
    return max(D_MODEL // CAST_ROWS_UP, D_FFN // CAST_ROWS_DOWN)


def _cast_jobs(big_f32, layer):
    w_in, w_out, w_gate, w_up, w_down = big_f32
    jobs = [(w_out, layer, CAST_ROWS_UP), (w_gate, layer, CAST_ROWS_UP), (w_up, layer, CAST_ROWS_UP),
            (w_down, layer, CAST_ROWS_DOWN)]
    if layer + 1 < w_in.shape[0]:
        jobs.append((w_in, layer + 1, CAST_ROWS_UP))
    return jobs


def _pr_call(x, nw, w_in, rot, s0, s0_shared, s_all, tables, tb, nseq, layer, cast_jobs=()):
    b, l, _ = x.shape
    nt = l // tb
    nb = (b // nseq) * nt
    cast_in, cast_out, cast_shapes = [], [], []
    for arr, lyr, rows in cast_jobs:
        shape = arr.shape[1:]
        assert shape[0] % rows == 0 and nb + 1 >= shape[0] // rows
        last = shape[0] // rows - 1
        cast_in.append(pl.BlockSpec((None, rows, shape[1]),
                                    lambda k, last=last, lyr=lyr: (lyr, jnp.minimum(k, last), 0)))
        cast_out.append(pl.BlockSpec((rows, shape[1]), lambda k, last=last: (jnp.minimum(k, last), 0)))
        cast_shapes.append(jax.ShapeDtypeStruct(shape, BF16))
    dmask, qdec, kdec, gblk = tables
    projected = lambda k: jnp.minimum(k, nb - 1)
    retained = lambda k: jnp.maximum(k - 1, 0)
    blk = lambda f, width: pl.BlockSpec((nseq, tb, width), lambda k: (f(k) // nt, f(k) % nt, 0))
    state_block = (None, nseq, RET_HEADS, HEAD_DIM, HEAD_DIM)
    if s0_shared:
        s0_spec = pl.BlockSpec(state_block, lambda k: (0, 0, 0, 0, 0))
    else:
        s0_spec = pl.BlockSpec(state_block, lambda k: (layer, retained(k) // nt, 0, 0, 0))
    outs = pl.pallas_call(
        functools.partial(_pr_kernel, nseq=nseq, tb=tb, nt=nt, nb=nb, n_cast=len(cast_jobs)),
        grid=(nb + 1,),
        in_specs=[
            blk(projected, D_MODEL),
            _layer_spec((1, D_MODEL), layer),
            _const_spec((D_MODEL, D_IN)),
            pl.BlockSpec((tb, 2 * HEAD_DIM), lambda k: (projected(k) % nt, 0)),
            s0_spec,
            pl.BlockSpec(memory_space=pl.ANY),
            _const_spec((RET_HEADS, tb, tb)),
            _const_spec((RET_HEADS, tb, HEAD_DIM)),
            _const_spec((RET_HEADS, tb, HEAD_DIM)),
            _const_spec((RET_HEADS, 1, HEAD_DIM)),
        ] + cast_in,
        out_specs=[
            blk(projected, D_SCAN),
            blk(retained, D_RET),
            pl.BlockSpec(state_block, lambda k: (layer, retained(k) // nt, 0, 0, 0)),
        ] + cast_out,
        out_shape=[
            jax.ShapeDtypeStruct((b, l, D_SCAN), F32),
            jax.ShapeDtypeStruct((b, l, D_RET), F32),
            jax.ShapeDtypeStruct(s_all.shape, F32),
        ] + cast_shapes,
        input_output_aliases={5: 2},
        scratch_shapes=[
            pltpu.VMEM((nseq, RET_HEADS, HEAD_DIM, HEAD_DIM), F32),
            pltpu.VMEM((nseq, tb, 3 * D_RET), BF16),
            pltpu.VMEM((nseq, tb, 3 * D_RET), BF16),
            pltpu.VMEM((nseq, tb, D_RET), F32),
            pltpu.VMEM((nseq, tb, D_RET), F32),
        ],
        compiler_params=pltpu.CompilerParams(
            dimension_semantics=("arbitrary",), vmem_limit_bytes=VMEM_LIMIT),
        name="pr",
    )(x, nw, w_in, rot, s0, s_all, dmask, qdec, kdec, gblk, *[job[0] for job in cast_jobs])
    return outs[:3], tuple(outs[3:])


def _s5_block(sc_ref, params, scratch, tt):
    bmat_ref, lam_ref, cmat_ref, dskip_ref = params[:4]
    tm, hbuf, hst, xpad, abuf, bbuf, lst, res = scratch

    for s in range(SEQ_TILE):
        for j in range(D_SCAN // LANES):
            tm[j, pl.ds(s, tt, stride=SEQ_TILE), :] = sc_ref[s, :, j * LANES:(j + 1) * LANES]

    u = jnp.concatenate([tm[0], tm[1]], axis=-1)
    hbuf[...] = jnp.dot(u.astype(BF16), bmat_ref[...], preferred_element_type=F32)

    lam_re = lam_ref[0:1, :]
    lam_im = lam_ref[1:2, :]
    hr = hst[:, 0:N_SSM]
    hi = hst[:, N_SSM:2 * N_SSM]
    for t in range(tt):
        lo = t * SEQ_TILE
        br = hbuf[lo:lo + SEQ_TILE, 0:N_SSM]
        bi = hbuf[lo:lo + SEQ_TILE, N_SSM:2 * N_SSM]
        hr, hi = lam_re * hr - lam_im * hi + br, lam_re * hi + lam_im * hr + bi
        hbuf[lo:lo + SEQ_TILE, 0:N_SSM] = hr
        hbuf[lo:lo + SEQ_TILE, N_SSM:2 * N_SSM] = hi
    hst[:, 0:N_SSM] = hr
    hst[:, N_SSM:2 * N_SSM] = hi

    y = jnp.dot(hbuf[...].astype(BF16), cmat_ref[...], preferred_element_type=F32) + dskip_ref[...] * u
    res[0] = y[:, :LANES]
    res[1] = y[:, LANES:]


def _lru_block(params, scratch, tt):
    convw_ref, convb_ref, wa_ref, ba_ref, wx_ref, bxb_ref, sp_ref = params[4:]
    tm, hbuf, hst, xpad, abuf, bbuf, lst, res = scratch
    rows = tt * SEQ_TILE
    pad_rows = (CONV_W - 1) * SEQ_TILE

    xpad[pad_rows:, :] = jnp.concatenate([tm[2], tm[3]], axis=-1)
    xc = convb_ref[...] + xpad[0:rows, :] * convw_ref[0:1, :]
    for j in range(1, CONV_W):
        xc = xc + xpad[j * SEQ_TILE:j * SEQ_TILE + rows, :] * convw_ref[j:j + 1, :]
    xcb = xc.astype(BF16)
    r = _sigmoid(jnp.dot(xcb, wa_ref[...], preferred_element_type=F32) + ba_ref[...])
    ig = _sigmoid(jnp.dot(xcb, wx_ref[...], preferred_element_type=F32) + bxb_ref[...])
    log_a = (-LRU_C) * r * sp_ref[...]
    a = jnp.exp(log_a)
    abuf[...] = a
    t1 = 1.0 - a * a
    bbuf[...] = jnp.where(t1 > 0.0, t1 * lax.rsqrt(t1), 0.0) * (ig * xc)

    hl = lst[...]
    for t in range(tt):
        lo = t * SEQ_TILE
        hl = abuf[lo:lo + SEQ_TILE, :] * hl + bbuf[lo:lo + SEQ_TILE, :]
        bbuf[lo:lo + SEQ_TILE, :] = hl
    lst[...] = hl
    for j in range(CONV_W - 1):
        xpad[j * SEQ_TILE:(j + 1) * SEQ_TILE, :] = xpad[rows + j * SEQ_TILE:rows + (j + 1) * SEQ_TILE, :]

    o_lru = bbuf[...] * jnp.concatenate([tm[4], tm[5]], axis=-1)
    res[2] = o_lru[:, :LANES]
    res[3] = o_lru[:, LANES:]


def _scan_out(scratch, osl_out, tt):
    res = scratch[7]
    for s in range(SEQ_TILE):
        for j in range((D_SSM + D_LRU) // LANES):
            osl_out[s * tt:(s + 1) * tt, j * LANES:(j + 1) * LANES] = res[j, pl.ds(s, tt, stride=SEQ_TILE), :]


def _merge_mixed(oret, osl, weights):
    wglu_ref, ms_ref = weights[:2]
    ga = jnp.dot(_gelu_tanh(osl[:, :D_SSM]).astype(BF16), wglu_ref[...], preferred_element_type=F32)
    o_ssm = ga[:, :D_SSM] * _sigmoid(ga[:, D_SSM:])
    mixed = jnp.concatenate(
        [_rms_rows(oret, EPS), _rms_rows(o_ssm, EPS), _rms_rows(osl[:, D_SSM:], EPS)], axis=-1) * ms_ref[...]
    return mixed.astype(BF16)


def _merge_project(x, mixed, weights):
    wo_ref, nf_ref = weights[2:4]
    x = x + jnp.dot(mixed, wo_ref[...], preferred_element_type=F32)
    return x, (_rms_rows(x, EPS) * nf_ref[...]).astype(BF16)


def _ffn_group(x, h, weights, lo, hi):
    wg_ref, wu_ref, wd_ref = weights[4:7]
    gate = jnp.dot(h, wg_ref[:, lo:hi], preferred_element_type=F32)
    up = jnp.dot(h, wu_ref[:, lo:hi], preferred_element_type=F32)
    ff = (gate * _sigmoid(gate) * up).astype(BF16)
    return x + jnp.dot(ff, wd_ref[lo:hi, :], preferred_element_type=F32)


def _mix_kernel(*refs, tt, nt, nb, final):
    sc_ref, x_ref, oret_ref, hs0_ref, hl0_ref, cc0_ref = refs[:6]
    scan_params = refs[6:17]
    merge_weights = refs[17:25]
    y_ref, hs_ref, hl_ref, cc_ref = refs[25:29]
    scratch = refs[29:37]
    osl, mixed = refs[37:39]
    tm, hbuf, hst, xpad, abuf, bbuf, lst, res = scratch
    k = pl.program_id(0)
    c = lax.rem(k, nt)
    rows = tt * SEQ_TILE

    @pl.when(k == 0)
    def _():
        mixed[...] = jnp.zeros(mixed.shape, BF16)

    @pl.when((c == 0) & (k < nb))
    def _():
        hst[...] = hs0_ref[...]
        lst[...] = hl0_ref[...]
        for j in range(CONV_W - 1):
            xpad[j * SEQ_TILE:(j + 1) * SEQ_TILE, :] = cc0_ref[j]

    _s5_block(sc_ref, scan_params, scratch, tt)

    x, h = _merge_project(x_ref[...].reshape(rows, D_MODEL), mixed[...], merge_weights)
    for gi, (lo, hi) in enumerate(FFN_SPLITS):
        x = _ffn_group(x, h, merge_weights, lo, hi)
        if gi == 0:
            _lru_block(scan_params, scratch, tt)
    if final:
        x = _rms_rows(x, EPS) * merge_weights[7][...]
    y_ref[...] = x.reshape(SEQ_TILE, tt, D_MODEL)

    _scan_out(scratch, osl, tt)
    mixed[...] = _merge_mixed(oret_ref[...].reshape(rows, D_RET), osl[...], merge_weights)

    @pl.when((c == nt - 1) & (k < nb))
    def _():
        hs_ref[...] = hst[...]
        hl_ref[...] = lst[...]
        for j in range(CONV_W - 1):
            cc_ref[j] = xpad[j * SEQ_TILE:(j + 1) * SEQ_TILE, :]


def _mix_call(sc, x, oret, hs0, hl0, cc0, sp, w, big, tt, layer, final):
    b, l, _ = sc.shape
    nt = l // tt
    nb = (b // SEQ_TILE) * nt
    rows = tt * SEQ_TILE
    scanned = lambda k: jnp.minimum(k, nb - 1)
    merged = lambda k: jnp.maximum(k - 1, 0)
    blk = lambda f, width: pl.BlockSpec((SEQ_TILE, tt, width), lambda k: (f(k) // nt, f(k) % nt, 0))
    seq_spec = lambda width: pl.BlockSpec((SEQ_TILE, width), lambda k: (scanned(k) // nt, 0))
    cc_spec = pl.BlockSpec((CONV_W - 1, SEQ_TILE, D_LRU), lambda k: (0, scanned(k) // nt, 0))
    seq0_spec = lambda width: pl.BlockSpec((None, SEQ_TILE, width), lambda k: (layer, scanned(k) // nt, 0))
    cc0_spec = pl.BlockSpec((None, CONV_W - 1, SEQ_TILE, D_LRU), lambda k: (layer, 0, scanned(k) // nt, 0))
    return pl.pallas_call(
        functools.partial(_mix_kernel, tt=tt, nt=nt, nb=nb, final=final),
        grid=(nb + 1,),
        in_specs=[
            blk(scanned, D_SCAN), blk(merged, D_MODEL), blk(scanned, D_RET),
            seq0_spec(2 * N_SSM), seq0_spec(D_LRU), cc0_spec,
        ] + [_layer_spec(t.shape[1:], layer) for t in sp] + [
            _layer_spec((D_SSM, 2 * D_SSM), layer),
            _layer_spec((1, D_MODEL), layer), _const_spec((D_MODEL, D_MODEL)),
            _layer_spec((1, D_MODEL), layer),
            _const_spec((D_MODEL, D_FFN)), _const_spec((D_MODEL, D_FFN)), _const_spec((D_FFN, D_MODEL)),
            _const_spec((1, D_MODEL)),
        ],
        out_specs=[
            blk(merged, D_MODEL),
            seq_spec(2 * N_SSM), seq_spec(D_LRU), cc_spec,
        ],
        out_shape=[
            jax.ShapeDtypeStruct((b, l, D_MODEL), F32),
            jax.ShapeDtypeStruct((b, 2 * N_SSM), F32),
            jax.ShapeDtypeStruct((b, D_LRU), F32),
            jax.ShapeDtypeStruct((CONV_W - 1, b, D_LRU), F32),
        ],
        scratch_shapes=[
            pltpu.VMEM((D_SCAN // LANES, rows, LANES), F32),
            pltpu.VMEM((rows, 2 * N_SSM), F32),
            pltpu.VMEM((SEQ_TILE, 2 * N_SSM), F32),
            pltpu.VMEM((rows + (CONV_W - 1) * SEQ_TILE, D_LRU), F32),
            pltpu.VMEM((rows, D_LRU), F32),
            pltpu.VMEM((rows, D_LRU), F32),
            pltpu.VMEM((SEQ_TILE, D_LRU), F32),
            pltpu.VMEM(((D_SSM + D_LRU) // LANES, rows, LANES), F32),
            pltpu.VMEM((rows, D_SSM + D_LRU), F32),
            pltpu.VMEM((rows, D_MODEL), BF16),
        ],
        compiler_params=pltpu.CompilerParams(
            dimension_semantics=("arbitrary",), vmem_limit_bytes=VMEM_LIMIT),
        name="mix",
    )(sc, x, oret, hs0, hl0, cc0, *sp,
      w['w_glu'], w['mix_scale'], big[0], w['norm_ffn'], *big[1:], w['norm_final'])


def _block_diag(t):
    n, r, c = t.shape
    cols = jnp.swapaxes(t, 0, 1).reshape(r, n * c)
    on_diag = jnp.arange(n)[:, None] == (jnp.arange(n * c) // c)[None, :]
    return jnp.where(on_diag[:, None, :], cols[None], 0.0).reshape(n * r, n * c)


def _scan_params(a_re, a_im, b_re, b_im, c_re, c_im, d_skip, log_dt,
                 conv_w, conv_b, w_a, b_a, w_x, b_x, lam):
    dt = jnp.exp(log_dt)[:, None]
    mag = jnp.exp(a_re * dt)
    lb_re = mag * jnp.cos(a_im * dt)
    lb_im = mag * jnp.sin(a_im * dt)
    den = a_re * a_re + a_im * a_im
    f_re = ((lb_re - 1.0) * a_re + lb_im * a_im) / den
    f_im = (lb_im * a_re - (lb_re - 1.0) * a_im) / den
    bb_re = f_re[..., None] * b_re - f_im[..., None] * b_im
    bb_im = f_re[..., None] * b_im + f_im[..., None] * b_re
    bmat = jnp.concatenate([_block_diag(jnp.swapaxes(bb_re, 1, 2)), _block_diag(jnp.swapaxes(bb_im, 1, 2))],
                           axis=1).astype(BF16)
    cmat = jnp.concatenate([_block_diag(jnp.swapaxes(c_re, 1, 2)), _block_diag(jnp.swapaxes(-c_im, 1, 2))],
                           axis=0).astype(BF16)
    lam_bar = jnp.stack([lb_re.reshape(N_SSM), lb_im.reshape(N_SSM)])
    bd_lru = lambda t: _block_diag(t).astype(BF16)
    row = lambda t: t.reshape(1, -1)
    return (bmat, lam_bar, cmat, row(d_skip),
            conv_w, row(conv_b), bd_lru(w_a), row(b_a), bd_lru(w_x), row(b_x),
            row(jax.nn.softplus(-lam)))


def _rotary_table(pos_offset, l):
    pos = pos_offset + jnp.arange(l, dtype=jnp.int32)
    inv_freq = ROPE_BASE ** (-jnp.arange(0, HEAD_DIM, 2, dtype=F32) / HEAD_DIM)
    ang = pos.astype(F32)[:, None] * inv_freq[None, :]
    cos, sin = jnp.cos(ang), jnp.sin(ang)
    return jnp.concatenate([cos, cos, -sin, sin], axis=-1)


def _trunk(x, pos_offset, s_ret, s_ssm, s_lru, c_conv, w, sp, big_f32, big_bf16):
    b, l, _ = x.shape
    depth = s_ssm.shape[0]
    tb = min(RET_BLOCK, l)
    tt = min(SCAN_BLOCK, l)
    nseq = min(b, max(1, RET_ROWS // tb), RET_SEQS_MAX)
    assert l % tb == 0 and l % tt == 0 and b % SEQ_TILE == 0 and b % nseq == 0
    rot = _rotary_table(pos_offset, l)
    tables = _ret_tables(tb)
    cast_in_pr = big_bf16 is None and (b // nseq) * (l // tb) + 1 >= _cast_steps()
    if big_bf16 is None and not cast_in_pr:
        big_bf16 = [tuple(t[i].astype(BF16) for t in big_f32) for i in range(depth)]
    w_in_next = big_f32[0][0].astype(BF16) if cast_in_pr else None
    zero_ret = s_ret is None
    if zero_ret:
        s_ret = jnp.zeros((1, nseq, RET_HEADS, HEAD_DIM, HEAD_DIM), F32)
    ret_all = jnp.zeros((depth, b, RET_HEADS, HEAD_DIM, HEAD_DIM), F32)
    hs0 = jnp.concatenate([s_ssm[..., 0].reshape(depth, b, N_SSM), s_ssm[..., 1].reshape(depth, b, N_SSM)], axis=-1)
    cc0 = jnp.swapaxes(c_conv, 1, 2)
    xs = x
    big_out, new_hs, new_lru, new_cc = [], [], [], []
    for i in range(depth):
        w_in = w_in_next if cast_in_pr else big_bf16[i][0]
        (sc, oret, ret_all), cast = _pr_call(xs, w['norm_mix'], w_in, rot, s_ret, zero_ret, ret_all, tables,
                                             tb, nseq, i, _cast_jobs(big_f32, i) if cast_in_pr else ())
        if cast_in_pr:
            big_out.append((w_in,) + cast[:4])
            w_in_next = cast[4] if len(cast) > 4 else None
        else:
            big_out.append(big_bf16[i])
        xs, hs, hl, cc = _mix_call(sc, xs, oret, hs0, s_lru, cc0, sp, w, big_out[i][1:], tt, i,
                                   final=(i == depth - 1))
        new_hs.append(hs)
        new_lru.append(hl)
        new_cc.append(cc)
    hs = jnp.stack(new_hs)
    new_ssm = jnp.stack([hs[..., :N_SSM].reshape(depth, b, SSM_GROUPS, SSM_STATE),
                         hs[..., N_SSM:].reshape(depth, b, SSM_GROUPS, SSM_STATE)], axis=-1)
    return (xs, ret_all, new_ssm, jnp.stack(new_lru), jnp.swapaxes(jnp.stack(new_cc), 1, 2)), big_out


def _prepare_weights(norm_mix, mix_scale, ssm_w_glu, norm_ffn, norm_final):
    depth = norm_mix.shape[0]
    rows = lambda t: t.reshape(depth, 1, D_MODEL)
    return {
        'norm_mix': rows(norm_mix), 'mix_scale': rows(mix_scale), 'w_glu': ssm_w_glu.astype(BF16),
        'norm_ffn': rows(norm_ffn), 'norm_final': norm_final.reshape(1, D_MODEL),
    }


def kernel(x_prompt, x_sample, state_ret, state_ssm, state_lru, cache_conv, norm_mix, w_in, mix_scale, w_out, ssm_a_re, ssm_a_im, ssm_b_re, ssm_b_im, ssm_c_re, ssm_c_im, ssm_d, ssm_log_dt, ssm_w_glu, lru_conv_w, lru_conv_b, lru_w_a, lru_b_a, lru_w_x, lru_b_x, lru_lambda, norm_ffn, w_ffn_gate, w_ffn_up, w_ffn_down, norm_final):
    w = _prepare_weights(norm_mix, mix_scale, ssm_w_glu, norm_ffn, norm_final)
    depth = norm_mix.shape[0]
    sp = jax.vmap(_scan_params)(ssm_a_re, ssm_a_im, ssm_b_re, ssm_b_im, ssm_c_re, ssm_c_im, ssm_d, ssm_log_dt,
                                lru_conv_w, lru_conv_b, lru_w_a, lru_b_a, lru_w_x, lru_b_x, lru_lambda)
    big_f32 = (w_in, w_out, w_ffn_gate, w_ffn_up, w_ffn_down)
    bp = x_prompt.shape[0]
    z_ssm = jnp.zeros((depth, bp, SSM_GROUPS, SSM_STATE, 2), F32)
    z_lru = jnp.zeros((depth, bp, D_LRU), F32)
    z_conv = jnp.zeros((depth, bp, CONV_W - 1, D_LRU), F32)
    out_p, big_bf16 = _trunk(x_prompt, 0, None, z_ssm, z_lru, z_conv, w, sp, big_f32, None)
    out_s, _ = _trunk(x_sample, PAST_LEN, state_ret, state_ssm, state_lru, cache_conv, w, sp, big_f32, big_bf16)
    return (out_p[0], out_s[0]) + tuple(out_p[1:]) + tuple(out_s[1:])
```

```python
import functools
import math

import jax
import jax.numpy as jnp
from jax import lax
from jax.experimental import pallas as pl
from jax.experimental.pallas import tpu as pltpu

F32 = jnp.float32
BF16 = jnp.bfloat16

D_MODEL = 1024
D_RET = 512
D_SSM = 256
D_LRU = 256
RET_HEADS = 4
HEAD_DIM = 128
RET_CHUNK = 64
ROPE_BASE = 10000.0
SSM_GROUP = 16
SSM_GROUPS = 16
SSM_STATE = 64
N_SSM = SSM_GROUPS * SSM_STATE
LRU_BLOCKS = 4
CONV_W = 4
LRU_C = 8.0
D_FFN = 2816
D_IN = 4 * D_RET + D_SSM + 2 * D_LRU
D_SCAN = D_SSM + 2 * D_LRU
EPS = 1e-6
GN_EPS = 1e-5

LANES = 128
SEQ_TILE = 8
PROJ_SEQS = 2
CAST_ROWS_UP = 32
CAST_ROWS_DOWN = 128
RET_BLOCK = 256
RET_ROWS = 1024
RET_SEQS_MAX = 8
PAST_LEN = 1024
SCAN_BLOCK = 64
FFN_SPLITS = ((0, 1024), (1024, 2048), (2048, D_FFN))
VMEM_LIMIT = 60 * 1024 * 1024


def _const_spec(shape):
    nd = len(shape)
    return pl.BlockSpec(shape, lambda *_: (0,) * nd, pipeline_mode=pl.Buffered(1))


def _layer_spec(shape, layer):
    nd = len(shape)
    return pl.BlockSpec((None,) + tuple(shape), lambda *_: (layer,) + (0,) * nd, pipeline_mode=pl.Buffered(1))


def _sigmoid(x):
    return 0.5 * (1.0 + jnp.tanh(0.5 * x))


def _gelu_tanh(x):
    return 0.5 * x * (1.0 + jnp.tanh(math.sqrt(2.0 / math.pi) * (x + 0.044715 * (x * x * x))))


def _rms_rows(x, eps):
    return x * lax.rsqrt(jnp.mean(x * x, axis=-1, keepdims=True) + eps)


def _proj_block(x, nw_ref, w_ref, rot_ref, qkv_out, g_out, sc_ref, nseq, tb):
    cos2 = rot_ref[:, :HEAD_DIM]
    sin2 = rot_ref[:, HEAD_DIM:]
    q_scale = HEAD_DIM ** -0.5
    h = (_rms_rows(x, EPS) * nw_ref[...]).astype(BF16)
    for j0 in range(0, nseq, PROJ_SEQS):
        j1 = min(j0 + PROJ_SEQS, nseq)
        proj = jnp.dot(h[j0 * tb:j1 * tb], w_ref[...], preferred_element_type=F32)
        for j in range(j0, j1):
            pj = proj[(j - j0) * tb:(j - j0 + 1) * tb]
            for head in range(2 * RET_HEADS):
                lo = head * HEAD_DIM
                xh = pj[:, lo:lo + HEAD_DIM]
                r = xh * cos2 + pltpu.roll(xh, HEAD_DIM // 2, axis=1) * sin2
                if head < RET_HEADS:
                    r = r * q_scale
                qkv_out[j, :, lo:lo + HEAD_DIM] = r.astype(BF16)
            qkv_out[j, :, 2 * D_RET:3 * D_RET] = pj[:, 2 * D_RET:3 * D_RET].astype(BF16)
            gg = pj[:, 3 * D_RET:4 * D_RET]
            g_out[j] = gg * _sigmoid(gg)
            sc_ref[j, :, :D_SSM + D_LRU] = pj[:, 4 * D_RET:4 * D_RET + D_SSM + D_LRU]
            sc_ref[j, :, D_SSM + D_LRU:] = _gelu_tanh(pj[:, 4 * D_RET + D_SSM + D_LRU:])


def _ret_block(qkv, g, tables, s_scr, o_ref, nseq):
    dmask_ref, qdec_ref, kdec_ref, gblk_ref = tables
    for j in range(nseq):
        for h in range(RET_HEADS):
            lo = h * HEAD_DIM
            q = qkv[j, :, lo:lo + HEAD_DIM]
            k = qkv[j, :, D_RET + lo:D_RET + lo + HEAD_DIM]
            v = qkv[j, :, 2 * D_RET + lo:2 * D_RET + lo + HEAD_DIM]
            s = s_scr[j, h]
            scores = lax.dot_general(q, k, (((1,), (1,)), ((), ())), preferred_element_type=F32)
            a = (scores * dmask_ref[h]).astype(BF16)
            o = jnp.dot(a, v, preferred_element_type=F32)
            o = o + jnp.dot(q, s.astype(BF16), preferred_element_type=F32) * qdec_ref[h]
            kd = (k.astype(F32) * kdec_ref[h]).astype(BF16)
            u = lax.dot_general(kd, v, (((0,), (0,)), ((), ())), preferred_element_type=F32)
            s_scr[j, h] = gblk_ref[h] * s + u
            oc = o - jnp.mean(o, axis=-1, keepdims=True)
            on = oc * lax.rsqrt(jnp.mean(oc * oc, axis=-1, keepdims=True) + GN_EPS)
            o_ref[j, :, lo:lo + HEAD_DIM] = on * g[j, :, lo:lo + HEAD_DIM]


def _pr_kernel(*refs, nseq, tb, nt, nb, n_cast):
    x_ref, nw_ref, w_ref, rot_ref, s0_ref, s_all_ref, dmask_ref, qdec_ref, kdec_ref, gblk_ref = refs[:10]
    f32_refs = refs[10:10 + n_cast]
    sc_ref, o_ref, sfin_ref = refs[10 + n_cast:13 + n_cast]
    bf16_refs = refs[13 + n_cast:13 + 2 * n_cast]
    s_scr, qkv_cur, qkv_prev, g_cur, g_prev = refs[13 + 2 * n_cast:]
    for src, dst in zip(f32_refs, bf16_refs):
        dst[...] = src[...].astype(BF16)
    del s_all_ref
    k = pl.program_id(0)
    c_ret = lax.rem(jnp.maximum(k - 1, 0), nt)

    @pl.when(k == 0)
    def _():
        qkv_prev[...] = jnp.zeros(qkv_prev.shape, BF16)
        g_prev[...] = jnp.zeros(g_prev.shape, F32)

    @pl.when(c_ret == 0)
    def _():
        s_scr[...] = s0_ref[...]

    _proj_block(x_ref[...].reshape(nseq * tb, D_MODEL), nw_ref, w_ref, rot_ref, qkv_cur, g_cur, sc_ref, nseq, tb)
    _ret_block(qkv_prev, g_prev, (dmask_ref, qdec_ref, kdec_ref, gblk_ref), s_scr, o_ref, nseq)
    qkv_prev[...] = qkv_cur[...]
    g_prev[...] = g_cur[...]

    @pl.when((c_ret == nt - 1) & (k >= 1))
    def _():
        sfin_ref[...] = s_scr[...]


def _ret_tables(tb):
    h = jnp.arange(RET_HEADS, dtype=F32)
    gamma = 1.0 - 2.0 ** (-5.0 - h)
    log_g = jnp.log(gamma)
    idx = jnp.arange(tb, dtype=F32)
    chunk = jnp.arange(tb, dtype=jnp.int32) // RET_CHUNK
    diff = idx[:, None] - idx[None, :]
    same = chunk[:, None] == chunk[None, :]
    earlier = chunk[:, None] > chunk[None, :]
    expo = jnp.where(same, jnp.abs(diff), diff)
    dmask = jnp.where((same | earlier)[None], jnp.exp(log_g[:, None, None] * expo[None]), 0.0)
    qdec = jnp.exp(log_g[:, None] * (idx + 1.0)[None, :])
    kdec = jnp.exp(log_g[:, None] * (tb - 1.0 - idx)[None, :])
    gblk = jnp.exp(log_g * tb)
    wide = lambda t: jnp.broadcast_to(t[:, :, None], t.shape + (HEAD_DIM,))
    return dmask, wide(qdec), wide(kdec), jnp.broadcast_to(gblk[:, None, None], (RET_HEADS, 1, HEAD_DIM))


def _cast_steps():
    return max(D_MODEL // CAST_ROWS_UP, D_FFN // CAST_ROWS_DOWN)


def _cast_jobs(big_f32, layer):
    w_in, w_out, w_gate, w_up, w_down = big_f32
    jobs = [(w_out, layer, CAST_ROWS_UP), (w_gate, layer, CAST_ROWS_UP), (w_up, layer, CAST_ROWS_UP),
            (w_down, layer, CAST_ROWS_DOWN)]
    if layer + 1 < w_in.shape[0]:
        jobs.append((w_in, layer + 1, CAST_ROWS_UP))
    return jobs


def _pr_call(x, nw, w_in, rot, s0, s0_shared, s_all, tables, tb, nseq, layer, cast_jobs=()):
    b, l, _ = x.shape
    nt = l // tb
    nb = (b // nseq) * nt
    cast_in, cast_out, cast_shapes = [], [], []
    for arr, lyr, rows in cast_jobs:
        shape = arr.shape[1:]
        assert shape[0] % rows == 0 and nb + 1 >= shape[0] // rows
        last = shape[0] // rows - 1
        cast_in.append(pl.BlockSpec((None, rows, shape[1]),
                                    lambda k, last=last, lyr=lyr: (lyr, jnp.minimum(k, last), 0)))
        cast_out.append(pl.BlockSpec((rows, shape[1]), lambda k, last=last: (jnp.minimum(k, last), 0)))
        cast_shapes.append(jax.ShapeDtypeStruct(shape, BF16))
    dmask, qdec, kdec, gblk = tables
    projected = lambda k: jnp.minimum(k, nb - 1)
    retained = lambda k: jnp.maximum(k - 1, 0)
    blk = lambda f, width: pl.BlockSpec((nseq, tb, width), lambda k: (f(k) // nt, f(k) % nt, 0))
    state_block = (None, nseq, RET_HEADS, HEAD_DIM, HEAD_DIM)
    if s0_shared:
        s0_spec = pl.BlockSpec(state_block, lambda k: (0, 0, 0, 0, 0))
    else:
        s0_spec = pl.BlockSpec(state_block, lambda k: (layer, retained(k) // nt, 0, 0, 0))
    outs = pl.pallas_call(
        functools.partial(_pr_kernel, nseq=nseq, tb=tb, nt=nt, nb=nb, n_cast=len(cast_jobs)),
        grid=(nb + 1,),
        in_specs=[
            blk(projected, D_MODEL),
            _layer_spec((1, D_MODEL), layer),
            _const_spec((D_MODEL, D_IN)),
            pl.BlockSpec((tb, 2 * HEAD_DIM), lambda k: (projected(k) % nt, 0)),
            s0_spec,
            pl.BlockSpec(memory_space=pl.ANY),
            _const_spec((RET_HEADS, tb, tb)),
            _const_spec((RET_HEADS, tb, HEAD_DIM)),
            _const_spec((RET_HEADS, tb, HEAD_DIM)),
            _const_spec((RET_HEADS, 1, HEAD_DIM)),
        ] + cast_in,
        out_specs=[
            blk(projected, D_SCAN),
            blk(retained, D_RET),
            pl.BlockSpec(state_block, lambda k: (layer, retained(k) // nt, 0, 0, 0)),
        ] + cast_out,
        out_shape=[
            jax.ShapeDtypeStruct((b, l, D_SCAN), F32),
            jax.ShapeDtypeStruct((b, l, D_RET), F32),
            jax.ShapeDtypeStruct(s_all.shape, F32),
        ] + cast_shapes,
        input_output_aliases={5: 2},
        scratch_shapes=[
            pltpu.VMEM((nseq, RET_HEADS, HEAD_DIM, HEAD_DIM), F32),
            pltpu.VMEM((nseq, tb, 3 * D_RET), BF16),
            pltpu.VMEM((nseq, tb, 3 * D_RET), BF16),
            pltpu.VMEM((nseq, tb, D_RET), F32),
            pltpu.VMEM((nseq, tb, D_RET), F32),
        ],
        compiler_params=pltpu.CompilerParams(
            dimension_semantics=("arbitrary",), vmem_limit_bytes=VMEM_LIMIT),
        name="pr",
    )(x, nw, w_in, rot, s0, s_all, dmask, qdec, kdec, gblk, *[job[0] for job in cast_jobs])
    return outs[:3], tuple(outs[3:])


def _s5_block(sc_ref, params, scratch, tt):
    bmat_ref, lam_ref, cmat_ref, dskip_ref = params[:4]
    tm, hbuf, hst, xpad, abuf, bbuf, lst, res = scratch

    for s in range(SEQ_TILE):
        for j in range(D_SCAN // LANES):
            tm[j, pl.ds(s, tt, stride=SEQ_TILE), :] = sc_ref[s, :, j * LANES:(j + 1) * LANES]

    u = jnp.concatenate([tm[0], tm[1]], axis=-1)
    hbuf[...] = jnp.dot(u.astype(BF16), bmat_ref[...], preferred_element_type=F32)

    lam_re = lam_ref[0:1, :]
    lam_im = lam_ref[1:2, :]
    hr = hst[:, 0:N_SSM]
    hi = hst[:, N_SSM:2 * N_SSM]
    for t in range(tt):
        lo = t * SEQ_TILE
        br = hbuf[lo:lo + SEQ_TILE, 0:N_SSM]
        bi = hbuf[lo:lo + SEQ_TILE, N_SSM:2 * N_SSM]
        hr, hi = lam_re * hr - lam_im * hi + br, lam_re * hi + lam_im * hr + bi
        hbuf[lo:lo + SEQ_TILE, 0:N_SSM] = hr
        hbuf[lo:lo + SEQ_TILE, N_SSM:2 * N_SSM] = hi
    hst[:, 0:N_SSM] = hr
    hst[:, N_SSM:2 * N_SSM] = hi

    y = jnp.dot(hbuf[...].astype(BF16), cmat_ref[...], preferred_element_type=F32) + dskip_ref[...] * u
    res[0] = y[:, :LANES]
    res[1] = y[:, LANES:]


def _lru_block(params, scratch, tt):
    convw_ref, convb_ref, wa_ref, ba_ref, wx_ref, bxb_ref, sp_ref = params[4:]
    tm, hbuf, hst, xpad, abuf, bbuf, lst, res = scratch
    rows = tt * SEQ_TILE
    pad_rows = (CONV_W - 1) * SEQ_TILE

    xpad[pad_rows:, :] = jnp.concatenate([tm[2], tm[3]], axis=-1)
    xc = convb_ref[...] + xpad[0:rows, :] * convw_ref[0:1, :]
    for j in range(1, CONV_W):
        xc = xc + xpad[j * SEQ_TILE:j * SEQ_TILE + rows, :] * convw_ref[j:j + 1, :]
    xcb = xc.astype(BF16)
    r = _sigmoid(jnp.dot(xcb, wa_ref[...], preferred_element_type=F32) + ba_ref[...])
    ig = _sigmoid(jnp.dot(xcb, wx_ref[...], preferred_element_type=F32) + bxb_ref[...])
    log_a = (-LRU_C) * r * sp_ref[...]
    a = jnp.exp(log_a)
    abuf[...] = a
    t1 = 1.0 - a * a
    bbuf[...] = jnp.where(t1 > 0.0, t1 * lax.rsqrt(t1), 0.0) * (ig * xc)

    hl = lst[...]
    for t in range(tt):
        lo = t * SEQ_TILE
        hl = abuf[lo:lo + SEQ_TILE, :] * hl + bbuf[lo:lo + SEQ_TILE, :]
        bbuf[lo:lo + SEQ_TILE, :] = hl
    lst[...] = hl
    for j in range(CONV_W - 1):
        xpad[j * SEQ_TILE:(j + 1) * SEQ_TILE, :] = xpad[rows + j * SEQ_TILE:rows + (j + 1) * SEQ_TILE, :]

    o_lru = bbuf[...] * jnp.concatenate([tm[4], tm[5]], axis=-1)
    res[2] = o_lru[:, :LANES]
    res[3] = o_lru[:, LANES:]


def _scan_out(scratch, osl_out, tt):
    res = scratch[7]
    for s in range(SEQ_TILE):
        for j in range((D_SSM + D_LRU) // LANES):
            osl_out[s * tt:(s + 1) * tt, j * LANES:(j + 1) * LANES] = res[j, pl.ds(s, tt, stride=SEQ_TILE), :]


def _merge_mixed(oret, osl, weights):
    wglu_ref, ms_ref = weights[:2]
    ga = jnp.dot(_gelu_tanh(osl[:, :D_SSM]).astype(BF16), wglu_ref[...], preferred_element_type=F32)
    o_ssm = ga[:, :D_SSM] * _sigmoid(ga[:, D_SSM:])
    mixed = jnp.concatenate(
        [_rms_rows(oret, EPS), _rms_rows(o_ssm, EPS), _rms_rows(osl[:, D_SSM:], EPS)], axis=-1) * ms_ref[...]
    return mixed.astype(BF16)


def _merge_project(x, mixed, weights):
    wo_ref, nf_ref = weights[2:4]
    x = x + jnp.dot(mixed, wo_ref[...], preferred_element_type=F32)
    return x, (_rms_rows(x, EPS) * nf_ref[...]).astype(BF16)


def _ffn_group(x, h, weights, lo, hi):
    wg_ref, wu_ref, wd_ref = weights[4:7]
    gate = jnp.dot(h, wg_ref[:, lo:hi], preferred_element_type=F32)
    up = jnp.dot(h, wu_ref[:, lo:hi], preferred_element_type=F32)
    ff = (gate * _sigmoid(gate) * up).astype(BF16)
    return x + jnp.dot(ff, wd_ref[lo:hi, :], preferred_element_type=F32)


def _mix_kernel(*refs, tt, nt, nb, final):
    sc_ref, x_ref, oret_ref, hs0_ref, hl0_ref, cc0_ref = refs[:6]
    scan_params = refs[6:17]
    merge_weights = refs[17:25]
    y_ref, hs_ref, hl_ref, cc_ref = refs[25:29]
    scratch = refs[29:37]
    osl, mixed = refs[37:39]
    tm, hbuf, hst, xpad, abuf, bbuf, lst, res = scratch
    k = pl.program_id(0)
    c = lax.rem(k, nt)
    rows = tt * SEQ_TILE

    @pl.when(k == 0)
    def _():
        mixed[...] = jnp.zeros(mixed.shape, BF16)

    @pl.when((c == 0) & (k < nb))
    def _():
        hst[...] = hs0_ref[...]
        lst[...] = hl0_ref[...]
        for j in range(CONV_W - 1):
            xpad[j * SEQ_TILE:(j + 1) * SEQ_TILE, :] = cc0_ref[j]

    _s5_block(sc_ref, scan_params, scratch, tt)

    x, h = _merge_project(x_ref[...].reshape(rows, D_MODEL), mixed[...], merge_weights)
    for gi, (lo, hi) in enumerate(FFN_SPLITS):
        x = _ffn_group(x, h, merge_weights, lo, hi)
        if gi == 0:
            _lru_block(scan_params, scratch, tt)
        if gi == 1:
            _scan_out(scratch, osl, tt)
            mixed[...] = _merge_mixed(oret_ref[...].reshape(rows, D_RET), osl[...], merge_weights)
    if final:
        x = _rms_rows(x, EPS) * merge_weights[7][...]
    y_ref[...] = x.reshape(SEQ_TILE, tt, D_MODEL)

    @pl.when((c == nt - 1) & (k < nb))
    def _():
        hs_ref[...] = hst[...]
        hl_ref[...] = lst[...]
        for j in range(CONV_W - 1):
            cc_ref[j] = xpad[j * SEQ_TILE:(j + 1) * SEQ_TILE, :]


def _mix_call(sc, x, oret, hs0, hl0, cc0, sp, w, big, tt, layer, final):
    b, l, _ = sc.shape
    nt = l // tt
    nb = (b // SEQ_TILE) * nt
    rows = tt * SEQ_TILE
    scanned = lambda k: jnp.minimum(k, nb - 1)
    merged = lambda k: jnp.maximum(k - 1, 0)
    blk = lambda f, width: pl.BlockSpec((SEQ_TILE, tt, width), lambda k: (f(k) // nt, f(k) % nt, 0))
    seq_spec = lambda width: pl.BlockSpec((SEQ_TILE, width), lambda k: (scanned(k) // nt, 0))
    cc_spec = pl.BlockSpec((CONV_W - 1, SEQ_TILE, D_LRU), lambda k: (0, scanned(k) // nt, 0))
    seq0_spec = lambda width: pl.BlockSpec((None, SEQ_TILE, width), lambda k: (layer, scanned(k) // nt, 0))
    cc0_spec = pl.BlockSpec((None, CONV_W - 1, SEQ_TILE, D_LRU), lambda k: (layer, 0, scanned(k) // nt, 0))
    return pl.pallas_call(
        functools.partial(_mix_kernel, tt=tt, nt=nt, nb=nb, final=final),
        grid=(nb + 1,),
        in_specs=[
            blk(scanned, D_SCAN), blk(merged, D_MODEL), blk(scanned, D_RET),
            seq0_spec(2 * N_SSM), seq0_spec(D_LRU), cc0_spec,
        ] + [_layer_spec(t.shape[1:], layer) for t in sp] + [
            _layer_spec((D_SSM, 2 * D_SSM), layer),
            _layer_spec((1, D_MODEL), layer), _const_spec((D_MODEL, D_MODEL)),
            _layer_spec((1, D_MODEL), layer),
            _const_spec((D_MODEL, D_FFN)), _const_spec((D_MODEL, D_FFN)), _const_spec((D_FFN, D_MODEL)),
            _const_spec((1, D_MODEL)),
        ],
        out_specs=[
            blk(merged, D_MODEL),
            seq_spec(2 * N_SSM), seq_spec(D_LRU), cc_spec,
        ],
        out_shape=[
            jax.ShapeDtypeStruct((b, l, D_MODEL), F32),
            jax.ShapeDtypeStruct((b, 2 * N_SSM), F32),
            jax.ShapeDtypeStruct((b, D_LRU), F32),
            jax.ShapeDtypeStruct((CONV_W - 1, b, D_LRU), F32),
        ],
        scratch_shapes=[
            pltpu.VMEM((D_SCAN // LANES, rows, LANES), F32),
            pltpu.VMEM((rows, 2 * N_SSM), F32),
            pltpu.VMEM((SEQ_TILE, 2 * N_SSM), F32),
            pltpu.VMEM((rows + (CONV_W - 1) * SEQ_TILE, D_LRU), F32),
            pltpu.VMEM((rows, D_LRU), F32),
            pltpu.VMEM((rows, D_LRU), F32),
            pltpu.VMEM((SEQ_TILE, D_LRU), F32),
            pltpu.VMEM(((D_SSM + D_LRU) // LANES, rows, LANES), F32),
            pltpu.VMEM((rows, D_SSM + D_LRU), F32),
            pltpu.VMEM((rows, D_MODEL), BF16),
        ],
        compiler_params=pltpu.CompilerParams(
            dimension_semantics=("arbitrary",), vmem_limit_bytes=VMEM_LIMIT),
        name="mix",
    )(sc, x, oret, hs0, hl0, cc0, *sp,
      w['w_glu'], w['mix_scale'], big[0], w['norm_ffn'], *big[1:], w['norm_final'])


def _block_diag(t):
    n, r, c = t.shape
    cols = jnp.swapaxes(t, 0, 1).reshape(r, n * c)
    on_diag = jnp.arange(n)[:, None] == (jnp.arange(n * c) // c)[None, :]
    return jnp.where(on_diag[:, None, :], cols[None], 0.0).reshape(n * r, n * c)


def _scan_params(a_re, a_im, b_re, b_im, c_re, c_im, d_skip, log_dt,
                 conv_w, conv_b, w_a, b_a, w_x, b_x, lam):
    dt = jnp.exp(log_dt)[:, None]
    mag = jnp.exp(a_re * dt)
    lb_re = mag * jnp.cos(a_im * dt)
    lb_im = mag * jnp.sin(a_im * dt)
    den = a_re * a_re + a_im * a_im
    f_re = ((lb_re - 1.0) * a_re + lb_im * a_im) / den
    f_im = (lb_im * a_re - (lb_re - 1.0) * a_im) / den
    bb_re = f_re[..., None] * b_re - f_im[..., None] * b_im
    bb_im = f_re[..., None] * b_im + f_im[..., None] * b_re
    bmat = jnp.concatenate([_block_diag(jnp.swapaxes(bb_re, 1, 2)), _block_diag(jnp.swapaxes(bb_im, 1, 2))],
                           axis=1).astype(BF16)
    cmat = jnp.concatenate([_block_diag(jnp.swapaxes(c_re, 1, 2)), _block_diag(jnp.swapaxes(-c_im, 1, 2))],
                           axis=0).astype(BF16)
    lam_bar = jnp.stack([lb_re.reshape(N_SSM), lb_im.reshape(N_SSM)])
    bd_lru = lambda t: _block_diag(t).astype(BF16)
    row = lambda t: t.reshape(1, -1)
    return (bmat, lam_bar, cmat, row(d_skip),
            conv_w, row(conv_b), bd_lru(w_a), row(b_a), bd_lru(w_x), row(b_x),
            row(jax.nn.softplus(-lam)))


def _rotary_table(pos_offset, l):
    pos = pos_offset + jnp.arange(l, dtype=jnp.int32)
    inv_freq = ROPE_BASE ** (-jnp.arange(0, HEAD_DIM, 2, dtype=F32) / HEAD_DIM)
    ang = pos.astype(F32)[:, None] * inv_freq[None, :]
    cos, sin = jnp.cos(ang), jnp.sin(ang)
    return jnp.concatenate([cos, cos, -sin, sin], axis=-1)


def _trunk(x, pos_offset, s_ret, s_ssm, s_lru, c_conv, w, sp, big_f32, big_bf16):
    b, l, _ = x.shape
    depth = s_ssm.shape[0]
    tb = min(RET_BLOCK, l)
    tt = min(SCAN_BLOCK, l)
    nseq = min(b, max(1, RET_ROWS // tb), RET_SEQS_MAX)
    assert l % tb == 0 and l % tt == 0 and b % SEQ_TILE == 0 and b % nseq == 0
    rot = _rotary_table(pos_offset, l)
    tables = _ret_tables(tb)
    cast_in_pr = big_bf16 is None and (b // nseq) * (l // tb) + 1 >= _cast_steps()
    if big_bf16 is None and not cast_in_pr:
        big_bf16 = [tuple(t[i].astype(BF16) for t in big_f32) for i in range(depth)]
    w_in_next = big_f32[0][0].astype(BF16) if cast_in_pr else None
    zero_ret = s_ret is None
    if zero_ret:
        s_ret = jnp.zeros((1, nseq, RET_HEADS, HEAD_DIM, HEAD_DIM), F32)
    ret_all = jnp.zeros((depth, b, RET_HEADS, HEAD_DIM, HEAD_DIM), F32)
    hs0 = jnp.concatenate([s_ssm[..., 0].reshape(depth, b, N_SSM), s_ssm[..., 1].reshape(depth, b, N_SSM)], axis=-1)
    cc0 = jnp.swapaxes(c_conv, 1, 2)
    xs = x
    big_out, new_hs, new_lru, new_cc = [], [], [], []
    for i in range(depth):
        w_in = w_in_next if cast_in_pr else big_bf16[i][0]
        (sc, oret, ret_all), cast = _pr_call(xs, w['norm_mix'], w_in, rot, s_ret, zero_ret, ret_all, tables,
                                             tb, nseq, i, _cast_jobs(big_f32, i) if cast_in_pr else ())
        if cast_in_pr:
            big_out.append((w_in,) + cast[:4])
            w_in_next = cast[4] if len(cast) > 4 else None
        else:
            big_out.append(big_bf16[i])
        xs, hs, hl, cc = _mix_call(sc, xs, oret, hs0, s_lru, cc0, sp, w, big_out[i][1:], tt, i,
                                   final=(i == depth - 1))
        new_hs.append(hs)
        new_lru.append(hl)
        new_cc.append(cc)
    hs = jnp.stack(new_hs)
    new_ssm = jnp.stack([hs[..., :N_SSM].reshape(depth, b, SSM_GROUPS, SSM_STATE),
                         hs[..., N_SSM:].reshape(depth, b, SSM_GROUPS, SSM_STATE)], axis=-1)
    return (xs, ret_all, new_ssm, jnp.stack(new_lru), jnp.swapaxes(jnp.stack(new_cc), 1, 2)), big_out


def _prepare_weights(norm_mix, mix_scale, ssm_w_glu, norm_ffn, norm_final):
    depth = norm_mix.shape[0]
    rows = lambda t: t.reshape(depth, 1, D_MODEL)
    return {
        'norm_mix': rows(norm_mix), 'mix_scale': rows(mix_scale), 'w_glu': ssm_w_glu.astype(BF16),
        'norm_ffn': rows(norm_ffn), 'norm_final': norm_final.reshape(1, D_MODEL),
    }


def kernel(x_prompt, x_sample, state_ret, state_ssm, state_lru, cache_conv, norm_mix, w_in, mix_scale, w_out, ssm_a_re, ssm_a_im, ssm_b_re, ssm_b_im, ssm_c_re, ssm_c_im, ssm_d, ssm_log_dt, ssm_w_glu, lru_conv_w, lru_conv_b, lru_w_a, lru_b_a, lru_w_x, lru_b_x, lru_lambda, norm_ffn, w_ffn_gate, w_ffn_up, w_ffn_down, norm_final):
    w = _prepare_weights(norm_mix, mix_scale, ssm_w_glu, norm_ffn, norm_final)
    depth = norm_mix.shape[0]
    sp = jax.vmap(_scan_params)(ssm_a_re, ssm_a_im, ssm_b_re, ssm_b_im, ssm_c_re, ssm_c_im, ssm_d, ssm_log_dt,
                                lru_conv_w, lru_conv_b, lru_w_a, lru_b_a, lru_w_x, lru_b_x, lru_lambda)
    big_f32 = (w_in, w_out, w_ffn_gate, w_ffn_up, w_ffn_down)
    bp = x_prompt.shape[0]
    z_ssm = jnp.zeros((depth, bp, SSM_GROUPS, SSM_STATE, 2), F32)
    z_lru = jnp.zeros((depth, bp, D_LRU), F32)
    z_conv = jnp.zeros((depth, bp, CONV_W - 1, D_LRU), F32)
    out_p, big_bf16 = _trunk(x_prompt, 0, None, z_ssm, z_lru, z_conv, w, sp, big_f32, None)
    out_s, _ = _trunk(x_sample, PAST_LEN, state_ret, state_ssm, state_lru, cache_conv, w, sp, big_f32, big_bf16)
    return (out_p[0], out_s[0]) + tuple(out_p[1:]) + tuple(out_s[1:])
```
